```python
import math
import jax
import jax.numpy as jnp
from jax import lax
import numpy as np

D_MODEL = 1024
BATCH = 8
SEQ = 4096
DEPTH = 1

MEM_LEN = 256
D_MIX = D_MODEL
D_S5 = D_MIX // 2
S5_GROUP = 16
S5_GROUPS = D_S5 // S5_GROUP
S5_STATE = 64
D_LRU = D_MIX - D_S5
LRU_HEADS = 8
LRU_HEAD_DIM = D_LRU // LRU_HEADS
LRU_CONV = 4
LRU_C = 8.0
D_FF = ((8 * D_MODEL // 3 + 127) // 128) * 128
FFN_CONV = 3
XA_HEADS = 4
XA_HEAD_DIM = D_MODEL // XA_HEADS
EPS = 1e-6

kernel_name = "hybrid_s5_rglru_xattn_convffn"


def rms_norm(x, g):
    xf = x.astype(jnp.float32)
    y = xf * lax.rsqrt(jnp.mean(xf * xf, axis=-1, keepdims=True) + EPS)
    return (y * g.astype(jnp.float32)).astype(x.dtype)


def causal_dwconv(x, w, b):
    k = w.shape[0]
    y = lax.conv_general_dilated(
        x, w[:, None, :].astype(x.dtype), window_strides=(1,),
        padding=[(k - 1, 0)], dimension_numbers=("NWC", "WIO", "NWC"),
        feature_group_count=x.shape[-1])
    return y + b.astype(x.dtype)


def _linear_scan_combine(left, right):
    a_l, b_l = left
    a_r, b_r = right
    return a_l * a_r, a_r * b_l + b_r


def s5_mixer(u, lam_re, lam_im, log_dt, b_re, b_im, c_re, c_im, d_skip, w_glu, b_glu):
    f32 = jnp.float32
    bsz, seq, _ = u.shape
    ug = u.astype(f32).reshape(bsz, seq, S5_GROUPS, S5_GROUP)
    lam = lax.complex(lam_re.astype(f32), lam_im.astype(f32))
    dt = jnp.exp(log_dt.astype(f32))[:, None]
    lam_bar = jnp.exp(lam * dt)
    b_mat = lax.complex(b_re.astype(f32), b_im.astype(f32))
    b_bar = ((lam_bar - 1.0) / lam)[..., None] * b_mat
    bu = jnp.einsum("gph,bsgh->bsgp", b_bar, ug.astype(jnp.complex64))
    a = jnp.broadcast_to(lam_bar, bu.shape)
    _, state = lax.associative_scan(_linear_scan_combine, (a, bu), axis=1)
    c_mat = lax.complex(c_re.astype(f32), c_im.astype(f32))
    y = jnp.real(jnp.einsum("ghp,bsgp->bsgh", c_mat, state)) + d_skip.astype(f32) * ug
    y = jax.nn.gelu(y.reshape(bsz, seq, D_S5))
    gate = jax.nn.sigmoid(y @ w_glu.astype(f32) + b_glu.astype(f32))
    return (y * gate).astype(u.dtype)


def rglru_mixer(xb, gb, conv_w, conv_b, w_a, b_a, w_x, b_x, lam):
    f32 = jnp.float32
    bsz, seq, _ = xb.shape
    xc = causal_dwconv(xb, conv_w, conv_b).astype(f32).reshape(bsz, seq, LRU_HEADS, LRU_HEAD_DIM)
    r = jax.nn.sigmoid(jnp.einsum("bshi,hij->bshj", xc, w_a.astype(f32)) + b_a.astype(f32))
    i = jax.nn.sigmoid(jnp.einsum("bshi,hij->bshj", xc, w_x.astype(f32)) + b_x.astype(f32))
    log_a = -LRU_C * r * jax.nn.softplus(-lam.astype(f32))
    a = jnp.exp(log_a)
    bx = jnp.sqrt(-jnp.expm1(2.0 * log_a)) * (i * xc)
    _, h = lax.associative_scan(_linear_scan_combine, (a, bx), axis=1)
    y = h.reshape(bsz, seq, D_LRU) * jax.nn.gelu(gb.astype(f32))
    return y.astype(xb.dtype)


def memory_cross_attention(h, mem_n, w_q, w_k, w_v, w_o):
    bsz, seq, _ = h.shape
    m = mem_n.shape[1]
    q = (h @ w_q).reshape(bsz, seq, XA_HEADS, XA_HEAD_DIM)
    k = (mem_n @ w_k).reshape(bsz, m, XA_HEADS, XA_HEAD_DIM)
    v = (mem_n @ w_v).reshape(bsz, m, XA_HEADS, XA_HEAD_DIM)
    scores = jnp.einsum("bshk,bmhk->bhsm", q.astype(jnp.float32), k.astype(jnp.float32)) * (XA_HEAD_DIM ** -0.5)
    p = jax.nn.softmax(scores, axis=-1).astype(v.dtype)
    o = jnp.einsum("bhsm,bmhk->bshk", p, v).reshape(bsz, seq, D_MODEL)
    return o @ w_o


def conv_ffn(h, w_up, conv_w, conv_b, w_down):
    z = causal_dwconv(h @ w_up, conv_w, conv_b)
    val, gate = z[..., :D_FF], z[..., D_FF:]
    return (jax.nn.gelu(gate) * val) @ w_down


def setup_inputs(seed: int = 0) -> dict:
    key = jax.random.key(seed)
    ks = iter(jax.random.split(key, 48))
    f32 = jnp.float32
    L = DEPTH

    def nrm(shape, scale):
        return scale * jax.random.normal(next(ks), shape, f32)

    def gain(shape):
        return 1.0 + nrm(shape, 0.02)

    x = nrm((BATCH, SEQ, D_MODEL), 1.0)
    mem = nrm((BATCH, MEM_LEN, D_MODEL), 1.0)
    mem_norm_g = gain((D_MODEL,))
    ln_mix_g = gain((L, D_MODEL))
    w_in = nrm((L, D_MODEL, D_S5 + 2 * D_LRU), D_MODEL ** -0.5)
    w_out = nrm((L, D_MIX, D_MODEL), D_MIX ** -0.5)
    n = jnp.arange(S5_STATE, dtype=f32)
    s5_lam_re = -0.5 + nrm((L, S5_GROUPS, S5_STATE), 0.01)
    s5_lam_im = jnp.pi * n + nrm((L, S5_GROUPS, S5_STATE), 0.01)
    s5_log_dt = jax.random.uniform(next(ks), (L, S5_GROUPS), f32, math.log(1e-3), math.log(1e-1))
    s5_b_re = nrm((L, S5_GROUPS, S5_STATE, S5_GROUP), (2 * S5_GROUP) ** -0.5)
    s5_b_im = nrm((L, S5_GROUPS, S5_STATE, S5_GROUP), (2 * S5_GROUP) ** -0.5)
    s5_c_re = nrm((L, S5_GROUPS, S5_GROUP, S5_STATE), S5_STATE ** -0.5)
    s5_c_im = nrm((L, S5_GROUPS, S5_GROUP, S5_STATE), S5_STATE ** -0.5)
    s5_d = nrm((L, S5_GROUPS, S5_GROUP), 1.0)
    s5_w_glu = nrm((L, D_S5, D_S5), D_S5 ** -0.5)
    s5_b_glu = nrm((L, D_S5), 0.01)
    lru_conv_w = nrm((L, LRU_CONV, D_LRU), LRU_CONV ** -0.5)
    lru_conv_b = nrm((L, D_LRU), 0.01)
    lru_w_a = nrm((L, LRU_HEADS, LRU_HEAD_DIM, LRU_HEAD_DIM), LRU_HEAD_DIM ** -0.5)
    lru_b_a = nrm((L, LRU_HEADS, LRU_HEAD_DIM), 0.01)
    lru_w_x = nrm((L, LRU_HEADS, LRU_HEAD_DIM, LRU_HEAD_DIM), LRU_HEAD_DIM ** -0.5)
    lru_b_x = nrm((L, LRU_HEADS, LRU_HEAD_DIM), 0.01)
    a_c = jax.random.uniform(next(ks), (L, LRU_HEADS, LRU_HEAD_DIM), f32, 0.9, 0.999)
    p = a_c ** (1.0 / LRU_C)
    lru_lam = jnp.log(p) - jnp.log1p(-p)
    ln_xa_g = gain((L, D_MODEL))
    xa_w_q = nrm((L, D_MODEL, D_MODEL), D_MODEL ** -0.5)
    xa_w_k = nrm((L, D_MODEL, D_MODEL), D_MODEL ** -0.5)
    xa_w_v = nrm((L, D_MODEL, D_MODEL), D_MODEL ** -0.5)
    xa_w_o = nrm((L, D_MODEL, D_MODEL), D_MODEL ** -0.5)
    ln_ffn_g = gain((L, D_MODEL))
    ffn_w_up = nrm((L, D_MODEL, 2 * D_FF), D_MODEL ** -0.5)
    ffn_conv_w = nrm((L, FFN_CONV, 2 * D_FF), FFN_CONV ** -0.5)
    ffn_conv_b = nrm((L, 2 * D_FF), 0.01)
    ffn_w_down = nrm((L, D_FF, D_MODEL), D_FF ** -0.5)
    final_norm_g = gain((D_MODEL,))
    return {"x": x, "mem": mem, "mem_norm_g": mem_norm_g, "ln_mix_g": ln_mix_g,
            "w_in": w_in, "w_out": w_out,
            "s5_lam_re": s5_lam_re, "s5_lam_im": s5_lam_im, "s5_log_dt": s5_log_dt,
            "s5_b_re": s5_b_re, "s5_b_im": s5_b_im, "s5_c_re": s5_c_re, "s5_c_im": s5_c_im,
            "s5_d": s5_d, "s5_w_glu": s5_w_glu, "s5_b_glu": s5_b_glu,
            "lru_conv_w": lru_conv_w, "lru_conv_b": lru_conv_b, "lru_w_a": lru_w_a,
            "lru_b_a": lru_b_a, "lru_w_x": lru_w_x, "lru_b_x": lru_b_x, "lru_lam": lru_lam,
            "ln_xa_g": ln_xa_g, "xa_w_q": xa_w_q, "xa_w_k": xa_w_k, "xa_w_v": xa_w_v,
            "xa_w_o": xa_w_o, "ln_ffn_g": ln_ffn_g, "ffn_w_up": ffn_w_up,
            "ffn_conv_w": ffn_conv_w, "ffn_conv_b": ffn_conv_b, "ffn_w_down": ffn_w_down,
            "final_norm_g": final_norm_g}


def reference(x, mem, mem_norm_g, ln_mix_g, w_in, w_out,
              s5_lam_re, s5_lam_im, s5_log_dt, s5_b_re, s5_b_im, s5_c_re, s5_c_im,
              s5_d, s5_w_glu, s5_b_glu,
              lru_conv_w, lru_conv_b, lru_w_a, lru_b_a, lru_w_x, lru_b_x, lru_lam,
              ln_xa_g, xa_w_q, xa_w_k, xa_w_v, xa_w_o,
              ln_ffn_g, ffn_w_up, ffn_conv_w, ffn_conv_b, ffn_w_down,
              final_norm_g):
    mem_n = rms_norm(mem, mem_norm_g)
    for l in range(DEPTH):
        h = rms_norm(x, ln_mix_g[l])
        z = h @ w_in[l]
        u_s5 = z[..., :D_S5]
        x_lru = z[..., D_S5:D_S5 + D_LRU]
        g_lru = z[..., D_S5 + D_LRU:]
        y_s5 = s5_mixer(u_s5, s5_lam_re[l], s5_lam_im[l], s5_log_dt[l], s5_b_re[l], s5_b_im[l],
                        s5_c_re[l], s5_c_im[l], s5_d[l], s5_w_glu[l], s5_b_glu[l])
        y_lru = rglru_mixer(x_lru, g_lru, lru_conv_w[l], lru_conv_b[l], lru_w_a[l], lru_b_a[l],
                            lru_w_x[l], lru_b_x[l], lru_lam[l])
        x = x + jnp.concatenate([y_s5, y_lru], axis=-1) @ w_out[l]
        x = x + memory_cross_attention(rms_norm(x, ln_xa_g[l]), mem_n,
                                       xa_w_q[l], xa_w_k[l], xa_w_v[l], xa_w_o[l])
        x = x + conv_ffn(rms_norm(x, ln_ffn_g[l]), ffn_w_up[l], ffn_conv_w[l], ffn_conv_b[l], ffn_w_down[l])
    return rms_norm(x, final_norm_g)
```

```python
import functools
import math

import jax
import jax.numpy as jnp
from jax import lax
from jax.experimental import pallas as pl
from jax.experimental.pallas import tpu as pltpu

EPS = 1e-6
LRU_C = 8.0
S5_CHUNK = 16
FFN_CHUNK = 128
ATTN_ROWS = 1024
FFN_ROW_BLOCK = 128
S5_GROUP_BLOCK = 8
MIB = 1024 * 1024

F32 = jnp.float32
BF16 = jnp.bfloat16


def _gelu(x):
    return 0.5 * x * (1.0 + jnp.tanh(0.7978845608028654 * (x + 0.044715 * (x * x * x))))


def _rms_rows(x, g):
    ms = jnp.mean(x * x, axis=-1, keepdims=True)
    return x * lax.rsqrt(ms + EPS) * g


def _shift_lanes(v, lane, shift, fill):
    return jnp.where(lane >= shift, pltpu.roll(v, shift, 1), fill)


def _nt_dot(a, b):
    return lax.dot_general(a, b, (((1,), (1,)), ((), ())), preferred_element_type=F32)


def _dot(a, b):
    return jnp.dot(a, b, preferred_element_type=F32)


def _mix_in_kernel(x_ref, g_ref, w_ref, zt_ref):
    hn = _rms_rows(x_ref[...], g_ref[...]).astype(BF16)
    zt_ref[...] = _nt_dot(w_ref[...], hn)


def _s5_kernel(zt_ref, toep_ref, bpow_ref, cpow_ref, lam_ref, d_ref, yt_ref, *, n_levels):
    T, rows, C = zt_ref.shape
    H = rows // S5_GROUP_BLOCK
    P = bpow_ref.shape[1] // 2
    lane = lax.broadcasted_iota(jnp.int32, (P, C), 1)
    for gi in range(S5_GROUP_BLOCK):
        u32 = zt_ref[:, gi * H:(gi + 1) * H, :].reshape(T * H, C)
        u = u32.astype(BF16)
        gst = _dot(bpow_ref[gi], u)
        hr, hi = gst[:P], gst[P:]
        lam = lam_ref[gi]
        for k in range(n_levels):
            sh = 1 << k
            lr, li = lam[:P, k:k + 1], lam[P:, k:k + 1]
            rr = _shift_lanes(hr, lane, sh, 0.0)
            ri = _shift_lanes(hi, lane, sh, 0.0)
            hr, hi = hr + lr * rr - li * ri, hi + lr * ri + li * rr
        hr = _shift_lanes(hr, lane, 1, 0.0)
        hi = _shift_lanes(hi, lane, 1, 0.0)
        hst = jnp.concatenate([hr, hi], axis=0).astype(BF16)
        y = _dot(toep_ref[gi], u) + _dot(cpow_ref[gi], hst) + d_ref[gi] * u32
        yt_ref[:, gi * H:(gi + 1) * H, :] = _gelu(y).reshape(T, H, C)


def _lru_kernel(x0_ref, x1_ref, x2_ref, x3_ref, g_ref, y5_ref, xres_ref,
                cst_ref, gw_ref, wglu_ref, wout_ref, out_ref,
                hloc_ref, acum_ref, hcur_ref, acur_ref, hin_ref, *, T, n_levels):
    j = pl.program_id(1)
    N, C = hcur_ref.shape
    half = N // 2
    lane = lax.broadcasted_iota(jnp.int32, (N, C), 1)
    cst = cst_ref[...]

    def col(k):
        return cst[:, k:k + 1]

    @pl.when(j < T)
    def _local_scan():
        s = j

        def tap(ref, k):
            v = ref[...]
            if k == 0:
                return v
            return lax.cond(s >= k, lambda: v, lambda: _shift_lanes(v, lane, 1, 0.0))

        xc = (col(3) * tap(x0_ref, 0) + col(2) * tap(x1_ref, 1) + col(1) * tap(x2_ref, 2)
              + col(0) * tap(x3_ref, 3) + col(4))
        xb = xc.astype(BF16)
        pre_a = jnp.concatenate([_dot(gw_ref[0], xb[:half]), _dot(gw_ref[1], xb[half:])], axis=0)
        pre_x = jnp.concatenate([_dot(gw_ref[2], xb[:half]), _dot(gw_ref[3], xb[half:])], axis=0)
        r = jax.nn.sigmoid(pre_a + col(5))
        i = jax.nn.sigmoid(pre_x + col(6))
        log_a = -(col(7) * r)
        a = jnp.exp(log_a)
        bx = jnp.sqrt(-jnp.tanh(log_a) * (a * a + 1.0)) * (i * xc)

        @pl.when(s == 0)
        def _():
            hcur_ref[...] = bx
            acur_ref[...] = a

        @pl.when(s > 0)
        def _():
            hcur_ref[...] = a * hcur_ref[...] + bx
            acur_ref[...] = a * acur_ref[...]

        hloc_ref[s] = hcur_ref[...]
        acum_ref[s] = acur_ref[...]

        @pl.when(s == T - 1)
        def _carry_scan():
            hc = hcur_ref[...]
            ac = acur_ref[...]
            for k in range(n_levels):
                sh = 1 << k
                hs = _shift_lanes(hc, lane, sh, 0.0)
                as_ = _shift_lanes(ac, lane, sh, 1.0)
                hc = ac * hs + hc
                ac = ac * as_
            hin_ref[...] = _shift_lanes(hc, lane, 1, 0.0)

    @pl.when(j >= T)
    def _finish():
        s = j - T
        h = hloc_ref[s] + acum_ref[s] * hin_ref[...]
        y_lru = h * _gelu(g_ref[...])
        y5 = y5_ref[...]
        gate = jax.nn.sigmoid(_dot(wglu_ref[...], y5.astype(BF16)) + col(8))
        ycat = jnp.concatenate([y5 * gate, y_lru], axis=0).astype(BF16)
        out_t = _dot(wout_ref[...], ycat)
        out_ref[...] = xres_ref[...] + out_t.T


def _mem_kv_kernel(mem_ref, g_ref, wk_ref, wv_ref, k_ref, v_ref):
    mn = _rms_rows(mem_ref[...], g_ref[...]).astype(BF16)
    k_ref[...] = _dot(mn, wk_ref[...]).astype(BF16)
    v_ref[...] = _dot(mn, wv_ref[...]).astype(BF16)


def _attn_kernel(x_ref, g_ref, wq_ref, wo_ref, k_ref, v_ref, out_ref, *, heads):
    x = x_ref[...]
    hn = _rms_rows(x, g_ref[...]).astype(BF16)
    q = _dot(hn, wq_ref[...])
    dh = q.shape[1] // heads
    scale = dh ** -0.5
    outs = []
    for h in range(heads):
        sl = slice(h * dh, (h + 1) * dh)
        sc = _nt_dot(q[:, sl].astype(BF16), k_ref[:, sl]) * scale
        p = jnp.exp(sc - jnp.max(sc, axis=-1, keepdims=True))
        denom = jnp.sum(p, axis=-1, keepdims=True)
        outs.append(_dot(p.astype(BF16), v_ref[:, sl]) / denom)
    o = jnp.concatenate(outs, axis=1).astype(BF16)
    out_ref[...] = x + _dot(o, wo_ref[...])


def _ffn_kernel(x_ref, gf_ref, gfin_ref, wup_ref, cw_ref, wdn_ref, out_ref,
                ring_ref, prod_ref, *, chunks_per_seq):
    j = pl.program_id(0)
    F2, C = ring_ref.shape[1], ring_ref.shape[2]
    F = F2 // 2
    x = x_ref[...]
    hn = _rms_rows(x, gf_ref[...]).astype(BF16)
    slot = j % 3
    z = _nt_dot(wup_ref[...], hn)

    @pl.when(j < 2)
    def _halo():
        lane = lax.broadcasted_iota(jnp.int32, (F2, C), 1)
        zs = pltpu.roll(z, 1, 1)
        ring_ref[slot] = jnp.where(lane % chunks_per_seq == 0, 0.0, zs)

    @pl.when(j >= 2)
    def _main():
        ring_ref[slot] = z
        s1 = (j + 2) % 3
        s2 = (j + 1) % 3

        def conv(r0):
            rows = pl.ds(r0, FFN_ROW_BLOCK)
            cw = cw_ref[rows, :]
            return (cw[:, 2:3] * ring_ref[slot, rows, :] + cw[:, 1:2] * ring_ref[s1, rows, :]
                    + cw[:, 0:1] * ring_ref[s2, rows, :] + cw[:, 3:4])

        def body(i, carry):
            r0 = pl.multiple_of(i * FFN_ROW_BLOCK, FFN_ROW_BLOCK)
            val = conv(r0)
            gate = conv(r0 + F)
            prod_ref[pl.ds(r0, FFN_ROW_BLOCK), :] = (_gelu(gate) * val).astype(BF16)
            return carry

        lax.fori_loop(0, F // FFN_ROW_BLOCK, body, 0)
        out_t = _dot(wdn_ref[...], prod_ref[...])
        y = x + out_t.T
        out_ref[...] = _rms_rows(y, gfin_ref[...])


def _s5_matrices(lam_re, lam_im, log_dt, b_re, b_im, c_re, c_im, T, n_levels):
    hp = lax.Precision.HIGHEST
    G, P = lam_re.shape
    H = b_re.shape[-1]
    lam = lax.complex(lam_re.astype(F32), lam_im.astype(F32))
    dt = jnp.exp(log_dt.astype(F32))[:, None]
    lam_bar = jnp.exp(lam * dt)
    b_bar = ((lam_bar - 1.0) / lam)[..., None] * lax.complex(b_re.astype(F32), b_im.astype(F32))
    c_mat = lax.complex(c_re.astype(F32), c_im.astype(F32))
    tau = jnp.arange(T + 1, dtype=F32)
    lam_pow = jnp.exp((lam * dt)[..., None] * tau)
    kern = jnp.real(jnp.einsum("ghp,gpt,gpk->gthk", c_mat, lam_pow[..., :T], b_bar, precision=hp))
    t_idx = jnp.arange(T)
    diff = t_idx[:, None] - t_idx[None, :]
    toep = kern[:, jnp.clip(diff, 0, T - 1)]
    toep = jnp.where((diff >= 0)[None, :, :, None, None], toep, 0.0)
    toep = toep.transpose(0, 1, 3, 2, 4).reshape(G, T * H, T * H)
    bp = lam_pow[..., T - 1 - t_idx][..., None] * b_bar[:, :, None, :]
    bpow = jnp.stack([jnp.real(bp), jnp.imag(bp)], axis=1).reshape(G, 2 * P, T * H)
    cp = c_mat[:, None, :, :] * lam_pow[..., 1:].transpose(0, 2, 1)[:, :, None, :]
    cpow = jnp.stack([jnp.real(cp), -jnp.imag(cp)], axis=3).reshape(G, T * H, 2 * P)
    steps = (T * (2 ** jnp.arange(n_levels))).astype(F32)
    ls = jnp.exp((lam * dt)[..., None] * steps)
    lam_scan = jnp.concatenate([jnp.real(ls), jnp.imag(ls)], axis=1)
    return toep.astype(BF16), bpow.astype(BF16), cpow.astype(BF16), lam_scan.astype(F32)


def _block_diag_t(w):
    nh, di, dj = w.shape
    eye = jnp.eye(nh, dtype=w.dtype)
    return jnp.einsum("hij,hg->hjgi", w, eye).reshape(nh * dj, nh * di)


def _const_spec(shape):
    return pl.BlockSpec(shape, lambda *_: (0,) * len(shape), pipeline_mode=pl.Buffered(1))


def _layer(x, mem_k, mem_v, p):
    B, S, D = x.shape
    T = S5_CHUNK
    C = S // T
    n_levels = int(math.log2(C))
    assert C == 1 << n_levels and S % T == 0
    d_s5 = p["s5_w_glu"].shape[0]
    d_lru = p["lru_conv_b"].shape[0]
    assert d_s5 == d_lru
    n_in = d_s5 + 2 * d_lru
    G = p["s5_lam_re"].shape[0]
    H = d_s5 // G
    assert G % S5_GROUP_BLOCK == 0

    x3 = x.reshape(B, C, T * D)
    w_in_t = p["w_in"].T.astype(BF16)
    zt = pl.pallas_call(
        _mix_in_kernel,
        grid=(B, T),
        in_specs=[pl.BlockSpec((None, C, D), lambda b, s: (b, 0, s)),
                  _const_spec((1, D)), _const_spec((n_in, D))],
        out_specs=pl.BlockSpec((None, None, n_in, C), lambda b, s: (b, s, 0, 0)),
        out_shape=jax.ShapeDtypeStruct((B, T, n_in, C), F32),
        compiler_params=pltpu.CompilerParams(dimension_semantics=("arbitrary", "arbitrary"),
                                             vmem_limit_bytes=32 * MIB),
        name="mix_in",
    )(x3, p["ln_mix_g"].reshape(1, D), w_in_t)

    toep, bpow, cpow, lam_scan = _s5_matrices(
        p["s5_lam_re"], p["s5_lam_im"], p["s5_log_dt"], p["s5_b_re"], p["s5_b_im"],
        p["s5_c_re"], p["s5_c_im"], T, n_levels)
    P2 = bpow.shape[1]
    d_col = jnp.tile(p["s5_d"].astype(F32)[:, None, :], (1, T, 1)).reshape(G, T * H, 1)
    gb = S5_GROUP_BLOCK
    yt = pl.pallas_call(
        functools.partial(_s5_kernel, n_levels=n_levels),
        grid=(G // gb, B),
        in_specs=[pl.BlockSpec((None, T, gb * H, C), lambda g, b: (b, 0, g, 0)),
                  pl.BlockSpec((gb, T * H, T * H), lambda g, b: (g, 0, 0)),
                  pl.BlockSpec((gb, P2, T * H), lambda g, b: (g, 0, 0)),
                  pl.BlockSpec((gb, T * H, P2), lambda g, b: (g, 0, 0)),
                  pl.BlockSpec((gb, P2, n_levels), lambda g, b: (g, 0, 0)),
                  pl.BlockSpec((gb, T * H, 1), lambda g, b: (g, 0, 0))],
        out_specs=pl.BlockSpec((None, T, gb * H, C), lambda g, b: (b, 0, g, 0)),
        out_shape=jax.ShapeDtypeStruct((B, T, d_s5, C), F32),
        compiler_params=pltpu.CompilerParams(dimension_semantics=("arbitrary", "arbitrary"),
                                             vmem_limit_bytes=40 * MIB),
        name="s5",
    )(zt, toep, bpow, cpow, lam_scan, d_col)

    csp = LRU_C * jax.nn.softplus(-p["lru_lam"].astype(F32)).reshape(d_lru)
    cols = [p["lru_conv_w"][k] for k in range(4)] + [
        p["lru_conv_b"], p["lru_b_a"].reshape(d_lru), p["lru_b_x"].reshape(d_lru), csp, p["s5_b_glu"]]
    cst = jnp.stack([c.astype(F32) for c in cols] + [jnp.zeros((d_lru,), F32)] * 7, axis=1)
    half = d_lru // 2
    wa = _block_diag_t(p["lru_w_a"])
    wx = _block_diag_t(p["lru_w_x"])
    gw = jnp.stack([wa[:half, :half], wa[half:, half:], wx[:half, :half], wx[half:, half:]]).astype(BF16)
    wglu_t = p["s5_w_glu"].T.astype(BF16)
    wout_t = p["w_out"].T.astype(BF16)

    def tap_spec(k):
        def imap(b, j):
            s = jnp.minimum(j, T - 1)
            return (b, (s - k + T) % T, 1, 0)
        return pl.BlockSpec((None, None, d_lru, C), imap)

    def fin(j):
        return jnp.maximum(j - T, 0)

    x1 = pl.pallas_call(
        functools.partial(_lru_kernel, T=T, n_levels=n_levels),
        grid=(B, 2 * T),
        in_specs=[tap_spec(0), tap_spec(1), tap_spec(2), tap_spec(3),
                  pl.BlockSpec((None, None, d_lru, C), lambda b, j: (b, fin(j), 2, 0)),
                  pl.BlockSpec((None, None, d_s5, C), lambda b, j: (b, fin(j), 0, 0)),
                  pl.BlockSpec((None, C, D), lambda b, j: (b, 0, fin(j))),
                  _const_spec((d_lru, 16)), _const_spec((4, half, half)),
                  _const_spec((d_s5, d_s5)), _const_spec((D, d_s5 + d_lru))],
        out_specs=pl.BlockSpec((None, C, D), lambda b, j: (b, 0, fin(j))),
        out_shape=jax.ShapeDtypeStruct((B, C, T * D), F32),
        scratch_shapes=[pltpu.VMEM((T, d_lru, C), F32), pltpu.VMEM((T, d_lru, C), F32),
                        pltpu.VMEM((d_lru, C), F32), pltpu.VMEM((d_lru, C), F32),
                        pltpu.VMEM((d_lru, C), F32)],
        compiler_params=pltpu.CompilerParams(dimension_semantics=("arbitrary", "arbitrary"),
                                             vmem_limit_bytes=48 * MIB),
        name="lru_out",
    )(zt, zt, zt, zt, zt, yt, x3, cst, gw, wglu_t, wout_t)
    x1 = x1.reshape(B, S, D)

    heads = 4
    M = mem_k.shape[1]
    tm = min(ATTN_ROWS, S)
    x2 = pl.pallas_call(
        functools.partial(_attn_kernel, heads=heads),
        grid=(B, S // tm),
        in_specs=[pl.BlockSpec((None, tm, D), lambda b, i: (b, i, 0)),
                  _const_spec((1, D)), _const_spec((D, D)), _const_spec((D, D)),
                  pl.BlockSpec((None, M, D), lambda b, i: (b, 0, 0)),
                  pl.BlockSpec((None, M, D), lambda b, i: (b, 0, 0))],
        out_specs=pl.BlockSpec((None, tm, D), lambda b, i: (b, i, 0)),
        out_shape=jax.ShapeDtypeStruct((B, S, D), F32),
        compiler_params=pltpu.CompilerParams(dimension_semantics=("arbitrary", "arbitrary"),
                                             vmem_limit_bytes=48 * MIB),
        name="xattn",
    )(x1, p["ln_xa_g"].reshape(1, D), p["xa_w_q"].astype(BF16), p["xa_w_o"].astype(BF16), mem_k, mem_v)
    return x2


def _ffn_final(x, p, final_g):
    B, S, D = x.shape
    T = min(FFN_CHUNK, S)
    R = B * S // T
    cps = S // T
    F2 = p["ffn_w_up"].shape[1]
    F = F2 // 2
    assert F % FFN_ROW_BLOCK == 0
    xr = x.reshape(R, T * D)
    wup_t = p["ffn_w_up"].T.astype(BF16)
    wdn_t = p["ffn_w_down"].T.astype(BF16)
    cw = jnp.concatenate([p["ffn_conv_w"].astype(F32).T, p["ffn_conv_b"].astype(F32)[:, None]], axis=1)

    def pos(j):
        return (j + T - 2) % T

    out = pl.pallas_call(
        functools.partial(_ffn_kernel, chunks_per_seq=cps),
        grid=(T + 2,),
        in_specs=[pl.BlockSpec((R, D), lambda j: (0, pos(j))),
                  _const_spec((1, D)), _const_spec((1, D)), _const_spec((F2, D)),
                  _const_spec((F2, 4)), _const_spec((D, F))],
        out_specs=pl.BlockSpec((R, D), lambda j: (0, pos(jnp.maximum(j, 2)))),
        out_shape=jax.ShapeDtypeStruct((R, T * D), F32),
        scratch_shapes=[pltpu.VMEM((3, F2, R), F32), pltpu.VMEM((F, R), BF16)],
        compiler_params=pltpu.CompilerParams(dimension_semantics=("arbitrary",),
                                             vmem_limit_bytes=56 * MIB),
        name="conv_ffn",
    )(xr, p["ln_ffn_g"].reshape(1, D), final_g.reshape(1, D), wup_t, cw, wdn_t)
    return out.reshape(B, S, D)


def kernel(x, mem, mem_norm_g, ln_mix_g, w_in, w_out, s5_lam_re, s5_lam_im, s5_log_dt, s5_b_re, s5_b_im, s5_c_re, s5_c_im, s5_d, s5_w_glu, s5_b_glu, lru_conv_w, lru_conv_b, lru_w_a, lru_b_a, lru_w_x, lru_b_x, lru_lam, ln_xa_g, xa_w_q, xa_w_k, xa_w_v, xa_w_o, ln_ffn_g, ffn_w_up, ffn_conv_w, ffn_conv_b, ffn_w_down, final_norm_g):
    params = dict(ln_mix_g=ln_mix_g, w_in=w_in, w_out=w_out, s5_lam_re=s5_lam_re, s5_lam_im=s5_lam_im,
                  s5_log_dt=s5_log_dt, s5_b_re=s5_b_re, s5_b_im=s5_b_im, s5_c_re=s5_c_re, s5_c_im=s5_c_im,
                  s5_d=s5_d, s5_w_glu=s5_w_glu, s5_b_glu=s5_b_glu, lru_conv_w=lru_conv_w,
                  lru_conv_b=lru_conv_b, lru_w_a=lru_w_a, lru_b_a=lru_b_a, lru_w_x=lru_w_x, lru_b_x=lru_b_x,
                  lru_lam=lru_lam, ln_xa_g=ln_xa_g, xa_w_q=xa_w_q, xa_w_k=xa_w_k, xa_w_v=xa_w_v,
                  xa_w_o=xa_w_o, ln_ffn_g=ln_ffn_g, ffn_w_up=ffn_w_up, ffn_conv_w=ffn_conv_w,
                  ffn_conv_b=ffn_conv_b, ffn_w_down=ffn_w_down)
    depth = ln_mix_g.shape[0]
    assert depth == 1, "the ConvFFN kernel fuses the final norm, which needs a single layer"
    B, M, D = mem.shape
    p = {k: v[0] for k, v in params.items()}
    mem_k, mem_v = pl.pallas_call(
        _mem_kv_kernel,
        grid=(B,),
        in_specs=[pl.BlockSpec((None, M, D), lambda b: (b, 0, 0)),
                  _const_spec((1, D)), _const_spec((D, D)), _const_spec((D, D))],
        out_specs=[pl.BlockSpec((None, M, D), lambda b: (b, 0, 0)),
                   pl.BlockSpec((None, M, D), lambda b: (b, 0, 0))],
        out_shape=[jax.ShapeDtypeStruct((B, M, D), BF16), jax.ShapeDtypeStruct((B, M, D), BF16)],
        compiler_params=pltpu.CompilerParams(dimension_semantics=("arbitrary",),
                                             vmem_limit_bytes=32 * MIB),
        name="mem_kv",
    )(mem, mem_norm_g.reshape(1, D), p["xa_w_k"].astype(BF16), p["xa_w_v"].astype(BF16))
    x2 = _layer(x, mem_k, mem_v, p)
    return _ffn_final(x2, p, final_norm_g)
```

```python
import functools
import math

import jax
import jax.numpy as jnp
from jax import lax
from jax.experimental import pallas as pl
from jax.experimental.pallas import tpu as pltpu

EPS = 1e-6
LRU_C = 8.0
S5_CHUNK = 16
FFN_ROWS = 1024
FFN_COLS = 256
ATTN_ROWS = 1024
S5_GROUP_BLOCK = 8
SUBLANES = 8
MIB = 1024 * 1024

F32 = jnp.float32
BF16 = jnp.bfloat16


def _gelu(x):
    half = 0.5 * x
    return half + half * jnp.tanh(x * (0.7978845608028654 + (0.7978845608028654 * 0.044715) * (x * x)))


def _rms_rows(x, g):
    ms = jnp.mean(x * x, axis=-1, keepdims=True)
    return x * lax.rsqrt(ms + EPS) * g


def _shift_lanes(v, lane, shift, fill):
    return jnp.where(lane >= shift, pltpu.roll(v, shift, 1), fill)


def _nt_dot(a, b):
    return lax.dot_general(a, b, (((1,), (1,)), ((), ())), preferred_element_type=F32)


def _dot(a, b):
    return jnp.dot(a, b, preferred_element_type=F32)


def _mix_in_kernel(x_ref, g_ref, w_ref, zt_ref):
    x = x_ref[:, pl.program_id(2), :]
    hn = _rms_rows(x, g_ref[...]).astype(BF16)
    zt_ref[...] = _nt_dot(w_ref[...], hn)


def _s5_kernel(zt_ref, toep_ref, bpow_ref, cpow_ref, lam_ref, d_ref, yt_ref, *, n_levels):
    T, rows, C = zt_ref.shape
    H = rows // S5_GROUP_BLOCK
    P = bpow_ref.shape[1] // 2
    lane = lax.broadcasted_iota(jnp.int32, (P, C), 1)
    for gi in range(S5_GROUP_BLOCK):
        u32 = zt_ref[:, gi * H:(gi + 1) * H, :].reshape(T * H, C)
        u = u32.astype(BF16)
        gst = _dot(bpow_ref[gi], u)
        hr, hi = gst[:P], gst[P:]
        lam = lam_ref[gi]
        for k in range(n_levels):
            sh = 1 << k
            lr, li = lam[:P, k:k + 1], lam[P:, k:k + 1]
            rr = _shift_lanes(hr, lane, sh, 0.0)
            ri = _shift_lanes(hi, lane, sh, 0.0)
            hr, hi = hr + lr * rr - li * ri, hi + lr * ri + li * rr
        hr = _shift_lanes(hr, lane, 1, 0.0)
        hi = _shift_lanes(hi, lane, 1, 0.0)
        hst = jnp.concatenate([hr, hi], axis=0).astype(BF16)
        y = _dot(toep_ref[gi], u) + _dot(cpow_ref[gi], hst) + d_ref[gi] * u32
        yt_ref[:, gi * H:(gi + 1) * H, :] = _gelu(y).reshape(T, H, C)


def _lru_kernel(x0_ref, x1_ref, x2_ref, x3_ref, g_ref, y5_ref,
                cst_ref, gw_ref, wglu_ref, wout_ref, out_ref,
                hloc_ref, acum_ref, hcur_ref, acur_ref, hin_ref, *, T, n_levels):
    j = pl.program_id(1)
    N, C = hcur_ref.shape
    half = N // 2
    lane = lax.broadcasted_iota(jnp.int32, (N, C), 1)
    cst = cst_ref[...]

    def col(k):
        return cst[:, k:k + 1]

    @pl.when(j < T)
    def _local_scan():
        s = j

        def tap(ref, k):
            v = ref[...]
            if k == 0:
                return v
            return lax.cond(s >= k, lambda: v, lambda: _shift_lanes(v, lane, 1, 0.0))

        xc = (col(3) * tap(x0_ref, 0) + col(2) * tap(x1_ref, 1) + col(1) * tap(x2_ref, 2)
              + col(0) * tap(x3_ref, 3) + col(4))
        xb = xc.astype(BF16)
        pre_a = jnp.concatenate([_dot(gw_ref[0], xb[:half]), _dot(gw_ref[1], xb[half:])], axis=0)
        pre_x = jnp.concatenate([_dot(gw_ref[2], xb[:half]), _dot(gw_ref[3], xb[half:])], axis=0)
        r = jax.nn.sigmoid(pre_a + col(5))
        i = jax.nn.sigmoid(pre_x + col(6))
        log_a = -(col(7) * r)
        a = jnp.exp(log_a)
        bx = jnp.sqrt(-jnp.tanh(log_a) * (a * a + 1.0)) * (i * xc)

        @pl.when(s == 0)
        def _():
            hcur_ref[...] = bx
            acur_ref[...] = a

        @pl.when(s > 0)
        def _():
            hcur_ref[...] = a * hcur_ref[...] + bx
            acur_ref[...] = a * acur_ref[...]

        hloc_ref[s] = hcur_ref[...]
        acum_ref[s] = acur_ref[...]

        @pl.when(s == T - 1)
        def _carry_scan():
            hc = hcur_ref[...]
            ac = acur_ref[...]
            for k in range(n_levels):
                sh = 1 << k
                hs = _shift_lanes(hc, lane, sh, 0.0)
                as_ = _shift_lanes(ac, lane, sh, 1.0)
                hc = ac * hs + hc
                ac = ac * as_
            hin_ref[...] = _shift_lanes(hc, lane, 1, 0.0)

    @pl.when(j >= T)
    def _finish():
        s = j - T
        h = hloc_ref[s] + acum_ref[s] * hin_ref[...]
        y_lru = h * _gelu(g_ref[...])
        y5 = y5_ref[...]
        gate = jax.nn.sigmoid(_dot(wglu_ref[...], y5.astype(BF16)) + col(8))
        ycat = jnp.concatenate([y5 * gate, y_lru], axis=0).astype(BF16)
        out_t = _dot(wout_ref[...], ycat)
        out_ref[:, s % SUBLANES, :] = out_t.T


def _mem_kv_kernel(mem_ref, g_ref, wk_ref, wv_ref, k_ref, v_ref):
    mn = _rms_rows(mem_ref[...], g_ref[...]).astype(BF16)
    k_ref[...] = _dot(mn, wk_ref[...]).astype(BF16)
    v_ref[...] = _dot(mn, wv_ref[...]).astype(BF16)


def _attn_kernel(x_ref, mix_ref, g_ref, wq_ref, wo_ref, k_ref, v_ref, out_ref, *, heads):
    x = x_ref[...] + mix_ref[...]
    hn = _rms_rows(x, g_ref[...]).astype(BF16)
    q = _dot(hn, wq_ref[...])
    dh = q.shape[1] // heads
    scale = dh ** -0.5
    outs = []
    for h in range(heads):
        sl = slice(h * dh, (h + 1) * dh)
        sc = _nt_dot(q[:, sl].astype(BF16), k_ref[:, sl]) * scale
        p = jnp.exp(sc - jnp.max(sc, axis=-1, keepdims=True))
        denom = jnp.sum(p, axis=-1, keepdims=True)
        outs.append(_dot(p.astype(BF16), v_ref[:, sl]) / denom)
    o = jnp.concatenate(outs, axis=1).astype(BF16)
    out_ref[...] = x + _dot(o, wo_ref[...])


def _ffn_kernel(x_ref, gf_ref, gfin_ref, wup_ref, cw_ref, wdn_ref, out_ref, halo_ref, prod_ref):
    i = pl.program_id(1)
    M = x_ref.shape[0]
    F = prod_ref.shape[1]
    x = x_ref[...]
    hn = _rms_rows(x, gf_ref[...]).astype(BF16)

    @pl.when(i == 0)
    def _sequence_start():
        halo_ref[...] = jnp.zeros_like(halo_ref)

    row = lax.broadcasted_iota(jnp.int32, (8, FFN_COLS), 0)

    def conv(c0):
        cols = slice(c0, c0 + FFN_COLS)
        z = _dot(hn, wup_ref[:, cols])
        prev = halo_ref[:, cols]
        halo_ref[:, cols] = z[M - 8:]
        z1 = pltpu.roll(z, 1, 0)
        z2 = pltpu.roll(z, 2, 0)
        z1 = jnp.concatenate([jnp.where(row < 1, pltpu.roll(prev, 1, 0), z1[:8]), z1[8:]], axis=0)
        z2 = jnp.concatenate([jnp.where(row < 2, pltpu.roll(prev, 2, 0), z2[:8]), z2[8:]], axis=0)
        cw = cw_ref[:, cols]
        return cw[2:3] * z + cw[1:2] * z1 + cw[0:1] * z2 + cw[3:4]

    for f in range(F // FFN_COLS):
        val = conv(f * FFN_COLS)
        gate = conv(F + f * FFN_COLS)
        prod_ref[:, f * FFN_COLS:(f + 1) * FFN_COLS] = (_gelu(gate) * val).astype(BF16)
    y = x + _dot(prod_ref[...], wdn_ref[...])
    out_ref[...] = _rms_rows(y, gfin_ref[...])


def _s5_matrices(lam_re, lam_im, log_dt, b_re, b_im, c_re, c_im, T, n_levels):
    hp = lax.Precision.HIGHEST
    G, P = lam_re.shape
    H = b_re.shape[-1]
    lam = lax.complex(lam_re.astype(F32), lam_im.astype(F32))
    dt = jnp.exp(log_dt.astype(F32))[:, None]
    lam_bar = jnp.exp(lam * dt)
    b_bar = ((lam_bar - 1.0) / lam)[..., None] * lax.complex(b_re.astype(F32), b_im.astype(F32))
    c_mat = lax.complex(c_re.astype(F32), c_im.astype(F32))
    tau = jnp.arange(T + 1, dtype=F32)
    lam_pow = jnp.exp((lam * dt)[..., None] * tau)
    kern = jnp.real(jnp.einsum("ghp,gpt,gpk->gthk", c_mat, lam_pow[..., :T], b_bar, precision=hp))
    t_idx = jnp.arange(T)
    diff = t_idx[:, None] - t_idx[None, :]
    toep = kern[:, jnp.clip(diff, 0, T - 1)]
    toep = jnp.where((diff >= 0)[None, :, :, None, None], toep, 0.0)
    toep = toep.transpose(0, 1, 3, 2, 4).reshape(G, T * H, T * H)
    bp = lam_pow[..., T - 1 - t_idx][..., None] * b_bar[:, :, None, :]
    bpow = jnp.stack([jnp.real(bp), jnp.imag(bp)], axis=1).reshape(G, 2 * P, T * H)
    cp = c_mat[:, None, :, :] * lam_pow[..., 1:].transpose(0, 2, 1)[:, :, None, :]
    cpow = jnp.stack([jnp.real(cp), -jnp.imag(cp)], axis=3).reshape(G, T * H, 2 * P)
    steps = (T * (2 ** jnp.arange(n_levels))).astype(F32)
    ls = jnp.exp((lam * dt)[..., None] * steps)
    lam_scan = jnp.concatenate([jnp.real(ls), jnp.imag(ls)], axis=1)
    return toep.astype(BF16), bpow.astype(BF16), cpow.astype(BF16), lam_scan.astype(F32)


def _block_diag_t(w):
    nh, di, dj = w.shape
    eye = jnp.eye(nh, dtype=w.dtype)
    return jnp.einsum("hij,hg->hjgi", w, eye).reshape(nh * dj, nh * di)


def _const_spec(shape):
    return pl.BlockSpec(shape, lambda *_: (0,) * len(shape), pipeline_mode=pl.Buffered(1))


def _layer(x, mem_k, mem_v, p):
    B, S, D = x.shape
    T = S5_CHUNK
    C = S // T
    n_levels = int(math.log2(C))
    assert C == 1 << n_levels and S % T == 0
    d_s5 = p["s5_w_glu"].shape[0]
    d_lru = p["lru_conv_b"].shape[0]
    assert d_s5 == d_lru
    n_in = d_s5 + 2 * d_lru
    G = p["s5_lam_re"].shape[0]
    H = d_s5 // G
    assert G % S5_GROUP_BLOCK == 0

    assert T % SUBLANES == 0
    x4 = x.reshape(B, C, T, D)
    w_in_t = p["w_in"].T.astype(BF16)
    zt = pl.pallas_call(
        _mix_in_kernel,
        grid=(B, T // SUBLANES, SUBLANES),
        in_specs=[pl.BlockSpec((None, C, SUBLANES, D), lambda b, h, i: (b, 0, h, 0)),
                  _const_spec((1, D)), _const_spec((n_in, D))],
        out_specs=pl.BlockSpec((None, None, n_in, C), lambda b, h, i: (b, h * SUBLANES + i, 0, 0)),
        out_shape=jax.ShapeDtypeStruct((B, T, n_in, C), F32),
        compiler_params=pltpu.CompilerParams(dimension_semantics=("arbitrary",) * 3,
                                             vmem_limit_bytes=32 * MIB),
        name="mix_in",
    )(x4, p["ln_mix_g"].reshape(1, D), w_in_t)

    toep, bpow, cpow, lam_scan = _s5_matrices(
        p["s5_lam_re"], p["s5_lam_im"], p["s5_log_dt"], p["s5_b_re"], p["s5_b_im"],
        p["s5_c_re"], p["s5_c_im"], T, n_levels)
    P2 = bpow.shape[1]
    d_col = jnp.tile(p["s5_d"].astype(F32)[:, None, :], (1, T, 1)).reshape(G, T * H, 1)
    gb = S5_GROUP_BLOCK
    yt = pl.pallas_call(
        functools.partial(_s5_kernel, n_levels=n_levels),
        grid=(G // gb, B),
        in_specs=[pl.BlockSpec((None, T, gb * H, C), lambda g, b: (b, 0, g, 0)),
                  pl.BlockSpec((gb, T * H, T * H), lambda g, b: (g, 0, 0)),
                  pl.BlockSpec((gb, P2, T * H), lambda g, b: (g, 0, 0)),
                  pl.BlockSpec((gb, T * H, P2), lambda g, b: (g, 0, 0)),
                  pl.BlockSpec((gb, P2, n_levels), lambda g, b: (g, 0, 0)),
                  pl.BlockSpec((gb, T * H, 1), lambda g, b: (g, 0, 0))],
        out_specs=pl.BlockSpec((None, T, gb * H, C), lambda g, b: (b, 0, g, 0)),
        out_shape=jax.ShapeDtypeStruct((B, T, d_s5, C), F32),
        compiler_params=pltpu.CompilerParams(dimension_semantics=("arbitrary", "arbitrary"),
                                             vmem_limit_bytes=40 * MIB),
        name="s5",
    )(zt, toep, bpow, cpow, lam_scan, d_col)

    csp = LRU_C * jax.nn.softplus(-p["lru_lam"].astype(F32)).reshape(d_lru)
    cols = [p["lru_conv_w"][k] for k in range(4)] + [
        p["lru_conv_b"], p["lru_b_a"].reshape(d_lru), p["lru_b_x"].reshape(d_lru), csp, p["s5_b_glu"]]
    cst = jnp.stack([c.astype(F32) for c in cols] + [jnp.zeros((d_lru,), F32)] * 7, axis=1)
    half = d_lru // 2
    wa = _block_diag_t(p["lru_w_a"])
    wx = _block_diag_t(p["lru_w_x"])
    gw = jnp.stack([wa[:half, :half], wa[half:, half:], wx[:half, :half], wx[half:, half:]]).astype(BF16)
    wglu_t = p["s5_w_glu"].T.astype(BF16)
    wout_t = p["w_out"].T.astype(BF16)

    def tap_spec(k):
        def imap(b, j):
            s = jnp.minimum(j, T - 1)
            return (b, (s - k + T) % T, 1, 0)
        return pl.BlockSpec((None, None, d_lru, C), imap)

    def fin(j):
        return jnp.maximum(j - T, 0)

    mix = pl.pallas_call(
        functools.partial(_lru_kernel, T=T, n_levels=n_levels),
        grid=(B, 2 * T),
        in_specs=[tap_spec(0), tap_spec(1), tap_spec(2), tap_spec(3),
                  pl.BlockSpec((None, None, d_lru, C), lambda b, j: (b, fin(j), 2, 0)),
                  pl.BlockSpec((None, None, d_s5, C), lambda b, j: (b, fin(j), 0, 0)),
                  _const_spec((d_lru, 16)), _const_spec((4, half, half)),
                  _const_spec((d_s5, d_s5)), _const_spec((D, d_s5 + d_lru))],
        out_specs=pl.BlockSpec((None, C, SUBLANES, D), lambda b, j: (b, 0, fin(j) // SUBLANES, 0)),
        out_shape=jax.ShapeDtypeStruct((B, C, T, D), F32),
        scratch_shapes=[pltpu.VMEM((T, d_lru, C), F32), pltpu.VMEM((T, d_lru, C), F32),
                        pltpu.VMEM((d_lru, C), F32), pltpu.VMEM((d_lru, C), F32),
                        pltpu.VMEM((d_lru, C), F32)],
        compiler_params=pltpu.CompilerParams(dimension_semantics=("arbitrary", "arbitrary"),
                                             vmem_limit_bytes=48 * MIB),
        name="lru_out",
    )(zt, zt, zt, zt, zt, yt, cst, gw, wglu_t, wout_t)
    mix = mix.reshape(B, S, D)

    heads = 4
    M = mem_k.shape[1]
    tm = min(ATTN_ROWS, S)
    x2 = pl.pallas_call(
        functools.partial(_attn_kernel, heads=heads),
        grid=(B, S // tm),
        in_specs=[pl.BlockSpec((None, tm, D), lambda b, i: (b, i, 0)),
                  pl.BlockSpec((None, tm, D), lambda b, i: (b, i, 0)),
                  _const_spec((1, D)), _const_spec((D, D)), _const_spec((D, D)),
                  pl.BlockSpec((None, M, D), lambda b, i: (b, 0, 0)),
                  pl.BlockSpec((None, M, D), lambda b, i: (b, 0, 0))],
        out_specs=pl.BlockSpec((None, tm, D), lambda b, i: (b, i, 0)),
        out_shape=jax.ShapeDtypeStruct((B, S, D), F32),
        compiler_params=pltpu.CompilerParams(dimension_semantics=("arbitrary", "arbitrary"),
                                             vmem_limit_bytes=48 * MIB),
        name="xattn",
    )(x, mix, p["ln_xa_g"].reshape(1, D), p["xa_w_q"].astype(BF16), p["xa_w_o"].astype(BF16), mem_k, mem_v)
    return x2


def _ffn_final(x, p, final_g):
    B, S, D = x.shape
    tm = min(FFN_ROWS, S)
    F2 = p["ffn_w_up"].shape[1]
    F = F2 // 2
    assert F % FFN_COLS == 0 and S % tm == 0
    cw = jnp.concatenate([p["ffn_conv_w"].astype(F32), p["ffn_conv_b"].astype(F32)[None, :]], axis=0)
    return pl.pallas_call(
        _ffn_kernel,
        grid=(B, S // tm),
        in_specs=[pl.BlockSpec((None, tm, D), lambda b, i: (b, i, 0)),
                  _const_spec((1, D)), _const_spec((1, D)), _const_spec((D, F2)),
                  _const_spec((4, F2)), _const_spec((F, D))],
        out_specs=pl.BlockSpec((None, tm, D), lambda b, i: (b, i, 0)),
        out_shape=jax.ShapeDtypeStruct((B, S, D), F32),
        scratch_shapes=[pltpu.VMEM((8, F2), F32), pltpu.VMEM((tm, F), BF16)],
        compiler_params=pltpu.CompilerParams(dimension_semantics=("arbitrary", "arbitrary"),
                                             vmem_limit_bytes=56 * MIB),
        name="conv_ffn",
    )(x, p["ln_ffn_g"].reshape(1, D), final_g.reshape(1, D), p["ffn_w_up"].astype(BF16), cw,
      p["ffn_w_down"].astype(BF16))


def kernel(x, mem, mem_norm_g, ln_mix_g, w_in, w_out, s5_lam_re, s5_lam_im, s5_log_dt, s5_b_re, s5_b_im, s5_c_re, s5_c_im, s5_d, s5_w_glu, s5_b_glu, lru_conv_w, lru_conv_b, lru_w_a, lru_b_a, lru_w_x, lru_b_x, lru_lam, ln_xa_g, xa_w_q, xa_w_k, xa_w_v, xa_w_o, ln_ffn_g, ffn_w_up, ffn_conv_w, ffn_conv_b, ffn_w_down, final_norm_g):
    params = dict(ln_mix_g=ln_mix_g, w_in=w_in, w_out=w_out, s5_lam_re=s5_lam_re, s5_lam_im=s5_lam_im,
                  s5_log_dt=s5_log_dt, s5_b_re=s5_b_re, s5_b_im=s5_b_im, s5_c_re=s5_c_re, s5_c_im=s5_c_im,
                  s5_d=s5_d, s5_w_glu=s5_w_glu, s5_b_glu=s5_b_glu, lru_conv_w=lru_conv_w,
                  lru_conv_b=lru_conv_b, lru_w_a=lru_w_a, lru_b_a=lru_b_a, lru_w_x=lru_w_x, lru_b_x=lru_b_x,
                  lru_lam=lru_lam, ln_xa_g=ln_xa_g, xa_w_q=xa_w_q, xa_w_k=xa_w_k, xa_w_v=xa_w_v,
                  xa_w_o=xa_w_o, ln_ffn_g=ln_ffn_g, ffn_w_up=ffn_w_up, ffn_conv_w=ffn_conv_w,
                  ffn_conv_b=ffn_conv_b, ffn_w_down=ffn_w_down)
    depth = ln_mix_g.shape[0]
    assert depth == 1, "the ConvFFN kernel fuses the final norm, which needs a single layer"
    B, M, D = mem.shape
    p = {k: v[0] for k, v in params.items()}
    mem_k, mem_v = pl.pallas_call(
        _mem_kv_kernel,
        grid=(B,),
        in_specs=[pl.BlockSpec((None, M, D), lambda b: (b, 0, 0)),
                  _const_spec((1, D)), _const_spec((D, D)), _const_spec((D, D))],
        out_specs=[pl.BlockSpec((None, M, D), lambda b: (b, 0, 0)),
                   pl.BlockSpec((None, M, D), lambda b: (b, 0, 0))],
        out_shape=[jax.ShapeDtypeStruct((B, M, D), BF16), jax.ShapeDtypeStruct((B, M, D), BF16)],
        compiler_params=pltpu.CompilerParams(dimension_semantics=("arbitrary",),
                                             vmem_limit_bytes=32 * MIB),
        name="mem_kv",
    )(mem, mem_norm_g.reshape(1, D), p["xa_w_k"].astype(BF16), p["xa_w_v"].astype(BF16))
    x2 = _layer(x, mem_k, mem_v, p)
    return _ffn_final(x2, p, final_norm_g)
```

```python
import functools
import math

import jax
import jax.numpy as jnp
from jax import lax
from jax.experimental import pallas as pl
from jax.experimental.pallas import tpu as pltpu

EPS = 1e-6
LRU_C = 8.0
S5_CHUNK = 16
FFN_ROWS = 1024
FFN_COLS = 256
ATTN_ROWS = 1024
S5_GROUP_BLOCK = 8
LRU_ROW_BLOCK = 32
SUBLANES = 8
MIB = 1024 * 1024

F32 = jnp.float32
BF16 = jnp.bfloat16


def _gelu(x):
    half = 0.5 * x
    return half + half * jnp.tanh(x * (0.7978845608028654 + (0.7978845608028654 * 0.044715) * (x * x)))


def _rms_rows(x, g):
    ms = jnp.mean(x * x, axis=-1, keepdims=True)
    return x * lax.rsqrt(ms + EPS) * g


def _shift_lanes(v, lane, shift, fill):
    return jnp.where(lane >= shift, pltpu.roll(v, shift, 1), fill)


def _nt_dot(a, b):
    return lax.dot_general(a, b, (((1,), (1,)), ((), ())), preferred_element_type=F32)


def _dot(a, b):
    return jnp.dot(a, b, preferred_element_type=F32)


def _mix_in_kernel(x_ref, g_ref, w_ref, zt_ref, hn_ref):
    q = pl.program_id(2)
    npos = zt_ref.shape[0]

    @pl.when(q == 0)
    def _normalise():
        xt = pltpu.einshape("cid->icd", x_ref[...])
        for i in range(SUBLANES):
            hn_ref[i] = _rms_rows(xt[i], g_ref[...]).astype(BF16)

    for i in range(npos):
        zt_ref[i] = _nt_dot(w_ref[...], hn_ref[q * npos + i])


def _s5_kernel(zt_ref, toep_ref, bpow_ref, cpow_ref, lam_ref, d_ref, yt_ref, *, n_levels):
    T, rows, C = zt_ref.shape
    H = rows // S5_GROUP_BLOCK
    P = bpow_ref.shape[1] // 2
    lane = lax.broadcasted_iota(jnp.int32, (P, C), 1)
    for gi in range(S5_GROUP_BLOCK):
        u32 = zt_ref[:, gi * H:(gi + 1) * H, :].reshape(T * H, C)
        u = u32.astype(BF16)
        gst = _dot(bpow_ref[gi], u)
        hr, hi = gst[:P], gst[P:]
        for k in range(n_levels):
            sh = 1 << k
            mr, mi = lam_ref[gi, 2 * k], lam_ref[gi, 2 * k + 1]
            rr = pltpu.roll(hr, sh, 1)
            ri = pltpu.roll(hi, sh, 1)
            hr, hi = hr + mr * rr - mi * ri, hi + mr * ri + mi * rr
        hr = _shift_lanes(hr, lane, 1, 0.0)
        hi = _shift_lanes(hi, lane, 1, 0.0)
        hst = jnp.concatenate([hr, hi], axis=0).astype(BF16)
        y = _dot(toep_ref[gi], u) + _dot(cpow_ref[gi], hst) + d_ref[gi] * u32
        yt_ref[:, gi * H:(gi + 1) * H, :] = _gelu(y).reshape(T, H, C)


def _lru_kernel(x0_ref, x1_ref, x2_ref, x3_ref, g_ref, y5_ref,
                cst_ref, gw_ref, wglu_ref, wout_ref, out_ref,
                hloc_ref, acum_ref, hcur_ref, acur_ref, hin_ref,
                xc_ref, xb_ref, prea_ref, prex_ref, *, T, n_levels):
    j = pl.program_id(1)
    N, C = hcur_ref.shape
    half = N // 2
    RB = LRU_ROW_BLOCK
    lane_rb = lax.broadcasted_iota(jnp.int32, (RB, C), 1)

    def row_loop(body):
        def step(i, carry):
            body(pl.multiple_of(i * RB, RB))
            return carry
        lax.fori_loop(0, N // RB, step, 0)

    @pl.when(j < T)
    def _local_scan():
        s = j

        def conv(first_positions):
            def body(r0):
                r = pl.ds(r0, RB)
                acc = cst_ref[3, r, :] * x0_ref[r, :] + cst_ref[4, r, :]
                for k, ref in ((1, x1_ref), (2, x2_ref), (3, x3_ref)):
                    v = ref[r, :]
                    if first_positions:
                        v = jnp.where(s >= k, v, _shift_lanes(v, lane_rb, 1, 0.0))
                    acc = acc + cst_ref[3 - k, r, :] * v
                xc_ref[r, :] = acc
                xb_ref[r, :] = acc.astype(BF16)
            return lambda: row_loop(body)

        lax.cond(s >= 3, conv(False), conv(True))
        prea_ref[:half] = _dot(gw_ref[0], xb_ref[:half])
        prea_ref[half:] = _dot(gw_ref[1], xb_ref[half:])
        prex_ref[:half] = _dot(gw_ref[2], xb_ref[:half])
        prex_ref[half:] = _dot(gw_ref[3], xb_ref[half:])

        @pl.when(s == 0)
        def _():
            hcur_ref[...] = jnp.zeros_like(hcur_ref)
            acur_ref[...] = jnp.ones_like(acur_ref)

        def scan_body(r0):
            r = pl.ds(r0, RB)
            ra = jax.nn.sigmoid(prea_ref[r, :] + cst_ref[5, r, :])
            ix = jax.nn.sigmoid(prex_ref[r, :] + cst_ref[6, r, :])
            log_a = -(cst_ref[7, r, :] * ra)
            a = jnp.exp(log_a)
            bx = jnp.sqrt(-jnp.tanh(log_a) * (a * a + 1.0)) * (ix * xc_ref[r, :])
            h = a * hcur_ref[r, :] + bx
            acc_a = a * acur_ref[r, :]
            hcur_ref[r, :] = h
            acur_ref[r, :] = acc_a
            hloc_ref[s, r, :] = h
            acum_ref[s, r, :] = acc_a

        row_loop(scan_body)

        @pl.when(s == T - 1)
        def _carry_scan():
            lane = lax.broadcasted_iota(jnp.int32, (N, C), 1)
            hc = hcur_ref[...]
            ac = acur_ref[...]
            for k in range(n_levels):
                sh = 1 << k
                hs = _shift_lanes(hc, lane, sh, 0.0)
                as_ = _shift_lanes(ac, lane, sh, 1.0)
                hc = ac * hs + hc
                ac = ac * as_
            hin_ref[...] = _shift_lanes(hc, lane, 1, 0.0)

    @pl.when(j >= T)
    def _finish():
        s = j - T
        h = hloc_ref[s] + acum_ref[s] * hin_ref[...]
        y_lru = h * _gelu(g_ref[...])
        y5 = y5_ref[...]
        gate = jax.nn.sigmoid(_dot(wglu_ref[...], y5.astype(BF16)) + cst_ref[8])
        ycat = jnp.concatenate([y5 * gate, y_lru], axis=0).astype(BF16)
        out_t = _dot(wout_ref[...], ycat)
        out_ref[...] = out_t.T


def _mem_kv_kernel(mem_ref, g_ref, wk_ref, wv_ref, k_ref, v_ref):
    mn = _rms_rows(mem_ref[...], g_ref[...]).astype(BF16)
    k_ref[...] = _dot(mn, wk_ref[...]).astype(BF16)
    v_ref[...] = _dot(mn, wv_ref[...]).astype(BF16)


def _attn_kernel(x_ref, mix_ref, g_ref, wq_ref, wo_ref, k_ref, v_ref, out_ref, *, heads):
    mix = pltpu.einshape("scd->csd", mix_ref[...])
    x = x_ref[...] + mix.reshape(x_ref.shape)
    hn = _rms_rows(x, g_ref[...]).astype(BF16)
    q = _dot(hn, wq_ref[...])
    dh = q.shape[1] // heads
    scale = dh ** -0.5
    outs = []
    for h in range(heads):
        sl = slice(h * dh, (h + 1) * dh)
        sc = _nt_dot(q[:, sl].astype(BF16), k_ref[:, sl]) * scale
        p = jnp.exp(sc - jnp.max(sc, axis=-1, keepdims=True))
        denom = jnp.sum(p, axis=-1, keepdims=True)
        outs.append(_dot(p.astype(BF16), v_ref[:, sl]) / denom)
    o = jnp.concatenate(outs, axis=1).astype(BF16)
    out_ref[...] = x + _dot(o, wo_ref[...])


def _ffn_kernel(x_ref, gf_ref, gfin_ref, wup_ref, cw_ref, wdn_ref, out_ref, halo_ref, prod_ref):
    i = pl.program_id(1)
    M = x_ref.shape[0]
    F = prod_ref.shape[1]
    x = x_ref[...]
    hn = _rms_rows(x, gf_ref[...]).astype(BF16)

    @pl.when(i == 0)
    def _sequence_start():
        halo_ref[...] = jnp.zeros_like(halo_ref)

    row = lax.broadcasted_iota(jnp.int32, (8, FFN_COLS), 0)

    def conv(c0):
        cols = slice(c0, c0 + FFN_COLS)
        z = _dot(hn, wup_ref[:, cols])
        prev = halo_ref[:, cols]
        halo_ref[:, cols] = z[M - 8:]
        z1 = pltpu.roll(z, 1, 0)
        z2 = pltpu.roll(z, 2, 0)
        z1 = jnp.concatenate([jnp.where(row < 1, pltpu.roll(prev, 1, 0), z1[:8]), z1[8:]], axis=0)
        z2 = jnp.concatenate([jnp.where(row < 2, pltpu.roll(prev, 2, 0), z2[:8]), z2[8:]], axis=0)
        cw = cw_ref[:, cols]
        return cw[2:3] * z + cw[1:2] * z1 + cw[0:1] * z2 + cw[3:4]

    for f in range(F // FFN_COLS):
        val = conv(f * FFN_COLS)
        gate = conv(F + f * FFN_COLS)
        prod_ref[:, f * FFN_COLS:(f + 1) * FFN_COLS] = (_gelu(gate) * val).astype(BF16)
    y = x + _dot(prod_ref[...], wdn_ref[...])
    out_ref[...] = _rms_rows(y, gfin_ref[...])


def _s5_matrices(lam_re, lam_im, log_dt, b_re, b_im, c_re, c_im, T, n_levels, C):
    hp = lax.Precision.HIGHEST
    G, P = lam_re.shape
    H = b_re.shape[-1]
    lam = lax.complex(lam_re.astype(F32), lam_im.astype(F32))
    ldt = lam * jnp.exp(log_dt.astype(F32))[:, None]
    lam_bar = jnp.exp(ldt)
    b_bar = ((lam_bar - 1.0) / lam)[..., None] * lax.complex(b_re.astype(F32), b_im.astype(F32))
    lam_pow = jnp.exp(ldt[..., None] * jnp.arange(T + 1, dtype=F32))
    rep = jnp.repeat(jnp.eye(T, dtype=F32), H, axis=1)
    til = jnp.tile(jnp.eye(H, dtype=F32), (1, T))

    def expand(a, m, spec):
        return (jnp.einsum(spec, jnp.real(a), m, precision=hp), jnp.einsum(spec, jnp.imag(a), m, precision=hp))

    def cmul(a, b):
        return a[0] * b[0] - a[1] * b[1], a[0] * b[1] + a[1] * b[0]

    bb = expand(b_bar, til, "gpk,kc->gpc")
    lam_rev = expand(lam_pow[..., :T][..., ::-1], rep, "gpt,tc->gpc")
    lam_fwd = expand(lam_pow[..., :T], rep, "gpt,tc->gpc")
    bpow = jnp.concatenate(cmul(lam_rev, bb), axis=1)
    mk = cmul(lam_fwd, bb)
    kall = (jnp.einsum("ghp,gpc->ghc", c_re.astype(F32), mk[0], precision=hp)
            - jnp.einsum("ghp,gpc->ghc", c_im.astype(F32), mk[1], precision=hp))
    col = jnp.arange(T * H)
    shift = ((col[None, :, None] % H == col[None, None, :] % H)
             & (col[None, :, None] // H == jnp.arange(T)[:, None, None] - col[None, None, :] // H)).astype(F32)
    toep = jnp.einsum("ghj,tjc->gthc", kall, shift, precision=hp).reshape(G, T * H, T * H)
    lam_t = lam_pow[..., 1:].transpose(0, 2, 1)
    lam_rows = expand(lam_t, rep.T, "gtp,ct->gcp")
    c_rows = (jnp.einsum("ch,ghp->gcp", til.T, c_re.astype(F32), precision=hp),
              jnp.einsum("ch,ghp->gcp", til.T, c_im.astype(F32), precision=hp))
    cp = cmul(c_rows, lam_rows)
    cpow = jnp.concatenate([cp[0], -cp[1]], axis=2)
    shifts = 2 ** jnp.arange(n_levels)
    ls = jnp.exp(ldt[..., None] * (T * shifts).astype(F32))
    ls = jnp.stack([jnp.real(ls), jnp.imag(ls)], axis=-1).transpose(0, 2, 3, 1)
    mask = (jnp.arange(C)[None, :] >= shifts[:, None]).astype(F32)
    lamm = (ls[..., None] * mask[None, :, None, None, :]).reshape(G, 2 * n_levels, P, C)
    return toep.astype(BF16), bpow.astype(BF16), cpow.astype(BF16), lamm


def _block_diag_t(w):
    nh, di, dj = w.shape
    eye = jnp.eye(nh, dtype=w.dtype)
    return jnp.einsum("hij,hg->hjgi", w, eye).reshape(nh * dj, nh * di)


def _const_spec(shape):
    return pl.BlockSpec(shape, lambda *_: (0,) * len(shape), pipeline_mode=pl.Buffered(1))


def _layer(x, mem_k, mem_v, p):
    B, S, D = x.shape
    T = S5_CHUNK
    C = S // T
    n_levels = int(math.log2(C))
    assert C == 1 << n_levels and S % T == 0
    d_s5 = p["s5_w_glu"].shape[0]
    d_lru = p["lru_conv_b"].shape[0]
    assert d_s5 == d_lru
    n_in = d_s5 + 2 * d_lru
    G = p["s5_lam_re"].shape[0]
    H = d_s5 // G
    assert G % S5_GROUP_BLOCK == 0

    assert T % SUBLANES == 0
    x4 = x.reshape(B, C, T, D)
    w_in_t = p["w_in"].T.astype(BF16)
    nsplit = 2
    npos = SUBLANES // nsplit
    zt = pl.pallas_call(
        _mix_in_kernel,
        grid=(B, T // SUBLANES, nsplit),
        in_specs=[pl.BlockSpec((None, C, SUBLANES, D), lambda b, h, q: (b, 0, h, 0)),
                  _const_spec((1, D)), _const_spec((n_in, D))],
        out_specs=pl.BlockSpec((None, npos, n_in, C), lambda b, h, q: (b, h * nsplit + q, 0, 0)),
        out_shape=jax.ShapeDtypeStruct((B, T, n_in, C), F32),
        scratch_shapes=[pltpu.VMEM((SUBLANES, C, D), BF16)],
        compiler_params=pltpu.CompilerParams(dimension_semantics=("arbitrary",) * 3,
                                             vmem_limit_bytes=56 * MIB),
        name="mix_in",
    )(x4, p["ln_mix_g"].reshape(1, D), w_in_t)

    toep, bpow, cpow, lamm = _s5_matrices(
        p["s5_lam_re"], p["s5_lam_im"], p["s5_log_dt"], p["s5_b_re"], p["s5_b_im"],
        p["s5_c_re"], p["s5_c_im"], T, n_levels, C)
    P2 = bpow.shape[1]
    d_rows = jnp.broadcast_to(jnp.tile(p["s5_d"].astype(F32), (1, T))[:, :, None], (G, T * H, C))
    gb = S5_GROUP_BLOCK
    yt = pl.pallas_call(
        functools.partial(_s5_kernel, n_levels=n_levels),
        grid=(G // gb, B),
        in_specs=[pl.BlockSpec((None, T, gb * H, C), lambda g, b: (b, 0, g, 0)),
                  pl.BlockSpec((gb, T * H, T * H), lambda g, b: (g, 0, 0)),
                  pl.BlockSpec((gb, P2, T * H), lambda g, b: (g, 0, 0)),
                  pl.BlockSpec((gb, T * H, P2), lambda g, b: (g, 0, 0)),
                  pl.BlockSpec((gb, 2 * n_levels, P2 // 2, C), lambda g, b: (g, 0, 0, 0)),
                  pl.BlockSpec((gb, T * H, C), lambda g, b: (g, 0, 0))],
        out_specs=pl.BlockSpec((None, T, gb * H, C), lambda g, b: (b, 0, g, 0)),
        out_shape=jax.ShapeDtypeStruct((B, T, d_s5, C), F32),
        compiler_params=pltpu.CompilerParams(dimension_semantics=("arbitrary", "arbitrary"),
                                             vmem_limit_bytes=48 * MIB),
        name="s5",
    )(zt, toep, bpow, cpow, lamm, d_rows)

    csp = LRU_C * jax.nn.softplus(-p["lru_lam"].astype(F32)).reshape(d_lru)
    cols = [p["lru_conv_w"][k] for k in range(4)] + [
        p["lru_conv_b"], p["lru_b_a"].reshape(d_lru), p["lru_b_x"].reshape(d_lru), csp, p["s5_b_glu"]]
    cst = jnp.broadcast_to(jnp.stack([c.astype(F32) for c in cols])[:, :, None], (len(cols), d_lru, C))
    half = d_lru // 2
    wa = _block_diag_t(p["lru_w_a"])
    wx = _block_diag_t(p["lru_w_x"])
    gw = jnp.stack([wa[:half, :half], wa[half:, half:], wx[:half, :half], wx[half:, half:]]).astype(BF16)
    wglu_t = p["s5_w_glu"].T.astype(BF16)
    wout_t = p["w_out"].T.astype(BF16)

    def tap_spec(k):
        def imap(b, j):
            s = jnp.minimum(j, T - 1)
            return (b, (s - k + T) % T, 1, 0)
        return pl.BlockSpec((None, None, d_lru, C), imap)

    def fin(j):
        return jnp.maximum(j - T, 0)

    mix = pl.pallas_call(
        functools.partial(_lru_kernel, T=T, n_levels=n_levels),
        grid=(B, 2 * T),
        in_specs=[tap_spec(0), tap_spec(1), tap_spec(2), tap_spec(3),
                  pl.BlockSpec((None, None, d_lru, C), lambda b, j: (b, fin(j), 2, 0)),
                  pl.BlockSpec((None, None, d_s5, C), lambda b, j: (b, fin(j), 0, 0)),
                  _const_spec((len(cols), d_lru, C)), _const_spec((4, half, half)),
                  _const_spec((d_s5, d_s5)), _const_spec((D, d_s5 + d_lru))],
        out_specs=pl.BlockSpec((None, None, C, D), lambda b, j: (b, fin(j), 0, 0)),
        out_shape=jax.ShapeDtypeStruct((B, T, C, D), F32),
        scratch_shapes=[pltpu.VMEM((T, d_lru, C), F32), pltpu.VMEM((T, d_lru, C), F32),
                        pltpu.VMEM((d_lru, C), F32), pltpu.VMEM((d_lru, C), F32),
                        pltpu.VMEM((d_lru, C), F32),
                        pltpu.VMEM((d_lru, C), F32), pltpu.VMEM((d_lru, C), BF16),
                        pltpu.VMEM((d_lru, C), F32), pltpu.VMEM((d_lru, C), F32)],
        compiler_params=pltpu.CompilerParams(dimension_semantics=("arbitrary", "arbitrary"),
                                             vmem_limit_bytes=48 * MIB),
        name="lru_out",
    )(zt, zt, zt, zt, zt, yt, cst, gw, wglu_t, wout_t)

    heads = 4
    M = mem_k.shape[1]
    tm = min(ATTN_ROWS, S)
    assert (tm // T) % SUBLANES == 0
    x2 = pl.pallas_call(
        functools.partial(_attn_kernel, heads=heads),
        grid=(B, S // tm),
        in_specs=[pl.BlockSpec((None, tm, D), lambda b, i: (b, i, 0)),
                  pl.BlockSpec((None, T, tm // T, D), lambda b, i: (b, 0, i, 0)),
                  _const_spec((1, D)), _const_spec((D, D)), _const_spec((D, D)),
                  pl.BlockSpec((None, M, D), lambda b, i: (b, 0, 0)),
                  pl.BlockSpec((None, M, D), lambda b, i: (b, 0, 0))],
        out_specs=pl.BlockSpec((None, tm, D), lambda b, i: (b, i, 0)),
        out_shape=jax.ShapeDtypeStruct((B, S, D), F32),
        compiler_params=pltpu.CompilerParams(dimension_semantics=("arbitrary", "arbitrary"),
                                             vmem_limit_bytes=48 * MIB),
        name="xattn",
    )(x, mix, p["ln_xa_g"].reshape(1, D), p["xa_w_q"].astype(BF16), p["xa_w_o"].astype(BF16), mem_k, mem_v)
    return x2


def _ffn_final(x, p, final_g):
    B, S, D = x.shape
    tm = min(FFN_ROWS, S)
    F2 = p["ffn_w_up"].shape[1]
    F = F2 // 2
    assert F % FFN_COLS == 0 and S % tm == 0
    cw = jnp.concatenate([p["ffn_conv_w"].astype(F32), p["ffn_conv_b"].astype(F32)[None, :]], axis=0)
    return pl.pallas_call(
        _ffn_kernel,
        grid=(B, S // tm),
        in_specs=[pl.BlockSpec((None, tm, D), lambda b, i: (b, i, 0)),
                  _const_spec((1, D)), _const_spec((1, D)), _const_spec((D, F2)),
                  _const_spec((4, F2)), _const_spec((F, D))],
        out_specs=pl.BlockSpec((None, tm, D), lambda b, i: (b, i, 0)),
        out_shape=jax.ShapeDtypeStruct((B, S, D), F32),
        scratch_shapes=[pltpu.VMEM((8, F2), F32), pltpu.VMEM((tm, F), BF16)],
        compiler_params=pltpu.CompilerParams(dimension_semantics=("arbitrary", "arbitrary"),
                                             vmem_limit_bytes=56 * MIB),
        name="conv_ffn",
    )(x, p["ln_ffn_g"].reshape(1, D), final_g.reshape(1, D), p["ffn_w_up"].astype(BF16), cw,
      p["ffn_w_down"].astype(BF16))


def kernel(x, mem, mem_norm_g, ln_mix_g, w_in, w_out, s5_lam_re, s5_lam_im, s5_log_dt, s5_b_re, s5_b_im, s5_c_re, s5_c_im, s5_d, s5_w_glu, s5_b_glu, lru_conv_w, lru_conv_b, lru_w_a, lru_b_a, lru_w_x, lru_b_x, lru_lam, ln_xa_g, xa_w_q, xa_w_k, xa_w_v, xa_w_o, ln_ffn_g, ffn_w_up, ffn_conv_w, ffn_conv_b, ffn_w_down, final_norm_g):
    params = dict(ln_mix_g=ln_mix_g, w_in=w_in, w_out=w_out, s5_lam_re=s5_lam_re, s5_lam_im=s5_lam_im,
                  s5_log_dt=s5_log_dt, s5_b_re=s5_b_re, s5_b_im=s5_b_im, s5_c_re=s5_c_re, s5_c_im=s5_c_im,
                  s5_d=s5_d, s5_w_glu=s5_w_glu, s5_b_glu=s5_b_glu, lru_conv_w=lru_conv_w,
                  lru_conv_b=lru_conv_b, lru_w_a=lru_w_a, lru_b_a=lru_b_a, lru_w_x=lru_w_x, lru_b_x=lru_b_x,
                  lru_lam=lru_lam, ln_xa_g=ln_xa_g, xa_w_q=xa_w_q, xa_w_k=xa_w_k, xa_w_v=xa_w_v,
                  xa_w_o=xa_w_o, ln_ffn_g=ln_ffn_g, ffn_w_up=ffn_w_up, ffn_conv_w=ffn_conv_w,
                  ffn_conv_b=ffn_conv_b, ffn_w_down=ffn_w_down)
    depth = ln_mix_g.shape[0]
    assert depth == 1, "the ConvFFN kernel fuses the final norm, which needs a single layer"
    B, M, D = mem.shape
    p = {k: v[0] for k, v in params.items()}
    mem_k, mem_v = pl.pallas_call(
        _mem_kv_kernel,
        grid=(B,),
        in_specs=[pl.BlockSpec((None, M, D), lambda b: (b, 0, 0)),
                  _const_spec((1, D)), _const_spec((D, D)), _const_spec((D, D))],
        out_specs=[pl.BlockSpec((None, M, D), lambda b: (b, 0, 0)),
                   pl.BlockSpec((None, M, D), lambda b: (b, 0, 0))],
        out_shape=[jax.ShapeDtypeStruct((B, M, D), BF16), jax.ShapeDtypeStruct((B, M, D), BF16)],
        compiler_params=pltpu.CompilerParams(dimension_semantics=("arbitrary",),
                                             vmem_limit_bytes=32 * MIB),
        name="mem_kv",
    )(mem, mem_norm_g.reshape(1, D), p["xa_w_k"].astype(BF16), p["xa_w_v"].astype(BF16))
    x2 = _layer(x, mem_k, mem_v, p)
    return _ffn_final(x2, p, final_norm_g)
```

```python
import functools
import math

import jax
import jax.numpy as jnp
from jax import lax
from jax.experimental import pallas as pl
from jax.experimental.pallas import tpu as pltpu

EPS = 1e-6
LRU_C = 8.0
S5_CHUNK = 16
FFN_ROWS = 1024
FFN_COLS = 256
ATTN_ROWS = 1024
S5_GROUP_BLOCK = 8
LRU_ROW_BLOCK = 32
SUBLANES = 8
LANES = 128
MIB = 1024 * 1024

F32 = jnp.float32
BF16 = jnp.bfloat16


def _gelu(x):
    half = 0.5 * x
    return half + half * jnp.tanh(x * (0.7978845608028654 + (0.7978845608028654 * 0.044715) * (x * x)))


def _sigmoid(x):
    return 0.5 * jnp.tanh(0.5 * x) + 0.5


def _roll_lanes(v, shift):
    C = v.shape[1]
    if shift % LANES == 0:
        return jnp.concatenate([v[:, C - shift:], v[:, :C - shift]], axis=1)
    return pltpu.roll(v, shift, 1)


def _rms_rows(x, g):
    ms = jnp.mean(x * x, axis=-1, keepdims=True)
    return x * lax.rsqrt(ms + EPS) * g


def _shift_lanes(v, lane, shift, fill):
    return jnp.where(lane >= shift, _roll_lanes(v, shift), fill)


def _nt_dot(a, b):
    return lax.dot_general(a, b, (((1,), (1,)), ((), ())), preferred_element_type=F32)


def _dot(a, b):
    return jnp.dot(a, b, preferred_element_type=F32)


def _mix_in_kernel(x_ref, g_ref, w_ref, zt_ref, hn_ref):
    q = pl.program_id(2)
    npos = zt_ref.shape[0]

    @pl.when(q == 0)
    def _first():
        xt = pltpu.einshape("cid->icd", x_ref[...])
        for i in range(SUBLANES):
            hn = _rms_rows(xt[i], g_ref[...]).astype(BF16)
            hn_ref[i] = hn
            if i < npos:
                zt_ref[i] = _nt_dot(w_ref[...], hn)

    @pl.when(q > 0)
    def _rest():
        for i in range(npos):
            zt_ref[i] = _nt_dot(w_ref[...], hn_ref[q * npos + i])


def _s5_kernel(zt_ref, toep_ref, bpow_ref, cpow_ref, lam_ref, d_ref, yt_ref, *, n_levels):
    T, rows, C = zt_ref.shape
    GB = S5_GROUP_BLOCK
    H = rows // GB
    P = bpow_ref.shape[1] // 2

    def group_input(gi):
        return zt_ref[:, gi * H:(gi + 1) * H, :].reshape(T * H, C)

    ends = [_dot(bpow_ref[gi], group_input(gi).astype(BF16)) for gi in range(GB)]
    hr = jnp.concatenate([e[:P] for e in ends], axis=0)
    hi = jnp.concatenate([e[P:] for e in ends], axis=0)
    for k in range(n_levels):
        mr, mi = lam_ref[2 * k], lam_ref[2 * k + 1]
        rr = _roll_lanes(hr, 1 << k)
        ri = _roll_lanes(hi, 1 << k)
        hr, hi = hr + mr * rr - mi * ri, hi + mr * ri + mi * rr
    lane = lax.broadcasted_iota(jnp.int32, (GB * P, C), 1)
    hr = _shift_lanes(hr, lane, 1, 0.0)
    hi = _shift_lanes(hi, lane, 1, 0.0)
    for gi in range(GB):
        u32 = group_input(gi)
        hst = jnp.concatenate([hr[gi * P:(gi + 1) * P], hi[gi * P:(gi + 1) * P]], axis=0).astype(BF16)
        y = _dot(toep_ref[gi], u32.astype(BF16)) + _dot(cpow_ref[gi], hst) + d_ref[gi] * u32
        yt_ref[:, gi * H:(gi + 1) * H, :] = _gelu(y).reshape(T, H, C)


def _lru_kernel(x0_ref, x1_ref, x2_ref, x3_ref, g_ref, y5_ref,
                cst_ref, gw_ref, wglu_ref, wout_ref, out_ref,
                hloc_ref, acum_ref, hcur_ref, acur_ref, hin_ref,
                xc_ref, xb_ref, prea_ref, prex_ref, ycat_ref, *, T, n_levels):
    j = pl.program_id(1)
    N, C = hcur_ref.shape
    half = N // 2
    RB = LRU_ROW_BLOCK
    lane_rb = lax.broadcasted_iota(jnp.int32, (RB, C), 1)
    slabs = [slice(r0, r0 + RB) for r0 in range(0, N, RB)]

    @pl.when(j < T)
    def _local_scan():
        s = j

        @pl.when(s == 0)
        def _():
            hcur_ref[...] = jnp.zeros_like(hcur_ref)
            acur_ref[...] = jnp.ones_like(acur_ref)

        def conv(r, first_positions):
            acc = cst_ref[3, r, :] * x0_ref[r, :] + cst_ref[4, r, :]
            for k, ref in ((1, x1_ref), (2, x2_ref), (3, x3_ref)):
                v = ref[r, :]
                if first_positions:
                    v = jnp.where(s >= k, v, _shift_lanes(v, lane_rb, 1, 0.0))
                acc = acc + cst_ref[3 - k, r, :] * v
            xc_ref[r, :] = acc
            xb_ref[r, :] = acc.astype(BF16)

        def scan(r):
            ra = _sigmoid(prea_ref[r, :] + cst_ref[5, r, :])
            ix = _sigmoid(prex_ref[r, :] + cst_ref[6, r, :])
            log_a = -(cst_ref[7, r, :] * ra)
            a = jnp.exp(log_a)
            bx = jnp.sqrt(-jnp.tanh(log_a) * (a * a + 1.0)) * (ix * xc_ref[r, :])
            h = a * hcur_ref[r, :] + bx
            acc_a = a * acur_ref[r, :]
            hcur_ref[r, :] = h
            acur_ref[r, :] = acc_a
            hloc_ref[s, r, :] = h
            acum_ref[s, r, :] = acc_a

        def position(first_positions):
            def run():
                for r in slabs:
                    conv(r, first_positions)
                prea_ref[:half] = _dot(gw_ref[0], xb_ref[:half])
                prex_ref[:half] = _dot(gw_ref[2], xb_ref[:half])
                prea_ref[half:] = _dot(gw_ref[1], xb_ref[half:])
                prex_ref[half:] = _dot(gw_ref[3], xb_ref[half:])
                for r in slabs:
                    scan(r)
            return run

        lax.cond(s >= 3, position(False), position(True))

        @pl.when(s == T - 1)
        def _carry_scan():
            lane = lax.broadcasted_iota(jnp.int32, (N, C), 1)
            hc = hcur_ref[...]
            ac = acur_ref[...]
            for k in range(n_levels):
                sh = 1 << k
                hs = _shift_lanes(hc, lane, sh, 0.0)
                as_ = _shift_lanes(ac, lane, sh, 1.0)
                hc = ac * hs + hc
                ac = ac * as_
            hin_ref[...] = _shift_lanes(hc, lane, 1, 0.0)
            ycat_ref[1] = jnp.zeros(ycat_ref.shape[1:], BF16)

    @pl.when(j >= T)
    def _finish():
        k = j - T
        out_t = _dot(wout_ref[...], ycat_ref[(k + 1) % 2])
        out_ref[...] = out_t.T
        s = jnp.minimum(k, T - 1)
        h = hloc_ref[s] + acum_ref[s] * hin_ref[...]
        y_lru = h * _gelu(g_ref[...])
        y5 = y5_ref[...]
        gate = _sigmoid(_dot(wglu_ref[...], y5.astype(BF16)) + cst_ref[8])
        ycat_ref[k % 2] = jnp.concatenate([y5 * gate, y_lru], axis=0).astype(BF16)


def _mem_kv_kernel(mem_ref, g_ref, wk_ref, wv_ref, k_ref, v_ref):
    mn = _rms_rows(mem_ref[...], g_ref[...]).astype(BF16)
    k_ref[...] = _dot(mn, wk_ref[...]).astype(BF16)
    v_ref[...] = _dot(mn, wv_ref[...]).astype(BF16)


def _attn_kernel(x_ref, mix_ref, g_ref, wq_ref, wo_ref, k_ref, v_ref, out_ref, *, heads):
    mix = pltpu.einshape("scd->csd", mix_ref[...])
    x = x_ref[...] + mix.reshape(x_ref.shape)
    hn = _rms_rows(x, g_ref[...]).astype(BF16)
    q = _dot(hn, wq_ref[...])
    dh = q.shape[1] // heads
    scale = dh ** -0.5
    outs = []
    for h in range(heads):
        sl = slice(h * dh, (h + 1) * dh)
        sc = _nt_dot(q[:, sl].astype(BF16), k_ref[:, sl]) * scale
        p = jnp.exp(sc - jnp.max(sc, axis=-1, keepdims=True))
        denom = jnp.sum(p, axis=-1, keepdims=True)
        outs.append(_dot(p.astype(BF16), v_ref[:, sl]) / denom)
    o = jnp.concatenate(outs, axis=1).astype(BF16)
    out_ref[...] = x + _dot(o, wo_ref[...])


def _ffn_kernel(x_ref, gf_ref, gfin_ref, wup_ref, cw_ref, wdn_ref, out_ref, halo_ref, prod_ref):
    i = pl.program_id(1)
    M = x_ref.shape[0]
    F = prod_ref.shape[1]
    x = x_ref[...]
    hn = _rms_rows(x, gf_ref[...]).astype(BF16)

    @pl.when(i == 0)
    def _sequence_start():
        halo_ref[...] = jnp.zeros_like(halo_ref)

    row = lax.broadcasted_iota(jnp.int32, (8, FFN_COLS), 0)

    def conv(c0):
        cols = slice(c0, c0 + FFN_COLS)
        z = _dot(hn, wup_ref[:, cols])
        prev = halo_ref[:, cols]
        halo_ref[:, cols] = z[M - 8:]
        z1 = pltpu.roll(z, 1, 0)
        z2 = pltpu.roll(z, 2, 0)
        z1 = jnp.concatenate([jnp.where(row < 1, pltpu.roll(prev, 1, 0), z1[:8]), z1[8:]], axis=0)
        z2 = jnp.concatenate([jnp.where(row < 2, pltpu.roll(prev, 2, 0), z2[:8]), z2[8:]], axis=0)
        cw = cw_ref[:, cols]
        return cw[2:3] * z + cw[1:2] * z1 + cw[0:1] * z2 + cw[3:4]

    for f in range(F // FFN_COLS):
        val = conv(f * FFN_COLS)
        gate = conv(F + f * FFN_COLS)
        prod_ref[:, f * FFN_COLS:(f + 1) * FFN_COLS] = (_gelu(gate) * val).astype(BF16)
    y = x + _dot(prod_ref[...], wdn_ref[...])
    out_ref[...] = _rms_rows(y, gfin_ref[...])


def _s5_matrices(lam_re, lam_im, log_dt, b_re, b_im, c_re, c_im, T, n_levels, C):
    hp = lax.Precision.HIGHEST
    G, P = lam_re.shape
    H = b_re.shape[-1]
    lam = lax.complex(lam_re.astype(F32), lam_im.astype(F32))
    ldt = lam * jnp.exp(log_dt.astype(F32))[:, None]
    lam_bar = jnp.exp(ldt)
    b_bar = ((lam_bar - 1.0) / lam)[..., None] * lax.complex(b_re.astype(F32), b_im.astype(F32))
    lam_pow = jnp.exp(ldt[..., None] * jnp.arange(T + 1, dtype=F32))
    rep = jnp.repeat(jnp.eye(T, dtype=F32), H, axis=1)
    til = jnp.tile(jnp.eye(H, dtype=F32), (1, T))

    def expand(a, m, spec):
        return (jnp.einsum(spec, jnp.real(a), m, precision=hp), jnp.einsum(spec, jnp.imag(a), m, precision=hp))

    def cmul(a, b):
        return a[0] * b[0] - a[1] * b[1], a[0] * b[1] + a[1] * b[0]

    bb = expand(b_bar, til, "gpk,kc->gpc")
    lam_rev = expand(lam_pow[..., :T][..., ::-1], rep, "gpt,tc->gpc")
    lam_fwd = expand(lam_pow[..., :T], rep, "gpt,tc->gpc")
    bpow = jnp.concatenate(cmul(lam_rev, bb), axis=1)
    mk = cmul(lam_fwd, bb)
    kall = (jnp.einsum("ghp,gpc->ghc", c_re.astype(F32), mk[0], precision=hp)
            - jnp.einsum("ghp,gpc->ghc", c_im.astype(F32), mk[1], precision=hp))
    col = jnp.arange(T * H)
    shift = ((col[None, :, None] % H == col[None, None, :] % H)
             & (col[None, :, None] // H == jnp.arange(T)[:, None, None] - col[None, None, :] // H)).astype(F32)
    toep = jnp.einsum("ghj,tjc->gthc", kall, shift, precision=hp).reshape(G, T * H, T * H)
    lam_t = lam_pow[..., 1:].transpose(0, 2, 1)
    lam_rows = expand(lam_t, rep.T, "gtp,ct->gcp")
    c_rows = (jnp.einsum("ch,ghp->gcp", til.T, c_re.astype(F32), precision=hp),
              jnp.einsum("ch,ghp->gcp", til.T, c_im.astype(F32), precision=hp))
    cp = cmul(c_rows, lam_rows)
    cpow = jnp.concatenate([cp[0], -cp[1]], axis=2)
    shifts = 2 ** jnp.arange(n_levels)
    ls = jnp.exp(ldt[..., None] * (T * shifts).astype(F32))
    gb = S5_GROUP_BLOCK
    ls = jnp.stack([jnp.real(ls), jnp.imag(ls)], axis=-1)
    ls = ls.reshape(G // gb, gb * P, 2 * n_levels).transpose(0, 2, 1)
    mask = (jnp.arange(C)[None, :] >= jnp.repeat(shifts, 2)[:, None]).astype(F32)
    lamm = ls[..., None] * mask[None, :, None, :]
    return toep.astype(BF16), bpow.astype(BF16), cpow.astype(BF16), lamm


def _block_diag_t(w):
    nh, di, dj = w.shape
    eye = jnp.eye(nh, dtype=w.dtype)
    return jnp.einsum("hij,hg->hjgi", w, eye).reshape(nh * dj, nh * di)


def _const_spec(shape):
    return pl.BlockSpec(shape, lambda *_: (0,) * len(shape), pipeline_mode=pl.Buffered(1))


def _layer(x, mem_k, mem_v, p):
    B, S, D = x.shape
    T = S5_CHUNK
    C = S // T
    n_levels = int(math.log2(C))
    assert C == 1 << n_levels and S % T == 0
    d_s5 = p["s5_w_glu"].shape[0]
    d_lru = p["lru_conv_b"].shape[0]
    assert d_s5 == d_lru
    n_in = d_s5 + 2 * d_lru
    G = p["s5_lam_re"].shape[0]
    H = d_s5 // G
    assert G % S5_GROUP_BLOCK == 0

    assert T % SUBLANES == 0
    x4 = x.reshape(B, C, T, D)
    w_in_t = p["w_in"].T.astype(BF16)
    nsplit = 2
    npos = SUBLANES // nsplit
    zt = pl.pallas_call(
        _mix_in_kernel,
        grid=(B, T // SUBLANES, nsplit),
        in_specs=[pl.BlockSpec((None, C, SUBLANES, D), lambda b, h, q: (b, 0, h, 0)),
                  _const_spec((1, D)), _const_spec((n_in, D))],
        out_specs=pl.BlockSpec((None, npos, n_in, C), lambda b, h, q: (b, h * nsplit + q, 0, 0)),
        out_shape=jax.ShapeDtypeStruct((B, T, n_in, C), F32),
        scratch_shapes=[pltpu.VMEM((SUBLANES, C, D), BF16)],
        compiler_params=pltpu.CompilerParams(dimension_semantics=("arbitrary",) * 3,
                                             vmem_limit_bytes=56 * MIB),
        name="mix_in",
    )(x4, p["ln_mix_g"].reshape(1, D), w_in_t)

    toep, bpow, cpow, lamm = _s5_matrices(
        p["s5_lam_re"], p["s5_lam_im"], p["s5_log_dt"], p["s5_b_re"], p["s5_b_im"],
        p["s5_c_re"], p["s5_c_im"], T, n_levels, C)
    P2 = bpow.shape[1]
    d_rows = jnp.broadcast_to(jnp.tile(p["s5_d"].astype(F32), (1, T))[:, :, None], (G, T * H, C))
    gb = S5_GROUP_BLOCK
    yt = pl.pallas_call(
        functools.partial(_s5_kernel, n_levels=n_levels),
        grid=(G // gb, B),
        in_specs=[pl.BlockSpec((None, T, gb * H, C), lambda g, b: (b, 0, g, 0)),
                  pl.BlockSpec((gb, T * H, T * H), lambda g, b: (g, 0, 0)),
                  pl.BlockSpec((gb, P2, T * H), lambda g, b: (g, 0, 0)),
                  pl.BlockSpec((gb, T * H, P2), lambda g, b: (g, 0, 0)),
                  pl.BlockSpec((None, 2 * n_levels, gb * P2 // 2, C), lambda g, b: (g, 0, 0, 0)),
                  pl.BlockSpec((gb, T * H, C), lambda g, b: (g, 0, 0))],
        out_specs=pl.BlockSpec((None, T, gb * H, C), lambda g, b: (b, 0, g, 0)),
        out_shape=jax.ShapeDtypeStruct((B, T, d_s5, C), F32),
        compiler_params=pltpu.CompilerParams(dimension_semantics=("arbitrary", "arbitrary"),
                                             vmem_limit_bytes=48 * MIB),
        name="s5",
    )(zt, toep, bpow, cpow, lamm, d_rows)

    csp = LRU_C * jax.nn.softplus(-p["lru_lam"].astype(F32)).reshape(d_lru)
    cols = [p["lru_conv_w"][k] for k in range(4)] + [
        p["lru_conv_b"], p["lru_b_a"].reshape(d_lru), p["lru_b_x"].reshape(d_lru), csp, p["s5_b_glu"]]
    cst = jnp.broadcast_to(jnp.stack([c.astype(F32) for c in cols])[:, :, None], (len(cols), d_lru, C))
    half = d_lru // 2
    wa = _block_diag_t(p["lru_w_a"])
    wx = _block_diag_t(p["lru_w_x"])
    gw = jnp.stack([wa[:half, :half], wa[half:, half:], wx[:half, :half], wx[half:, half:]]).astype(BF16)
    wglu_t = p["s5_w_glu"].T.astype(BF16)
    wout_t = p["w_out"].T.astype(BF16)

    def tap_spec(k):
        def imap(b, j):
            s = jnp.minimum(j, T - 1)
            return (b, (s - k + T) % T, 1, 0)
        return pl.BlockSpec((None, None, d_lru, C), imap)

    def mixed(j):
        return jnp.clip(j - T, 0, T - 1)

    def projected(j):
        return jnp.maximum(j - T - 1, 0)

    mix = pl.pallas_call(
        functools.partial(_lru_kernel, T=T, n_levels=n_levels),
        grid=(B, 2 * T + 1),
        in_specs=[tap_spec(0), tap_spec(1), tap_spec(2), tap_spec(3),
                  pl.BlockSpec((None, None, d_lru, C), lambda b, j: (b, mixed(j), 2, 0)),
                  pl.BlockSpec((None, None, d_s5, C), lambda b, j: (b, mixed(j), 0, 0)),
                  _const_spec((len(cols), d_lru, C)), _const_spec((4, half, half)),
                  _const_spec((d_s5, d_s5)), _const_spec((D, d_s5 + d_lru))],
        out_specs=pl.BlockSpec((None, None, C, D), lambda b, j: (b, projected(j), 0, 0)),
        out_shape=jax.ShapeDtypeStruct((B, T, C, D), F32),
        scratch_shapes=[pltpu.VMEM((T, d_lru, C), F32), pltpu.VMEM((T, d_lru, C), F32),
                        pltpu.VMEM((d_lru, C), F32), pltpu.VMEM((d_lru, C), F32),
                        pltpu.VMEM((d_lru, C), F32),
                        pltpu.VMEM((d_lru, C), F32), pltpu.VMEM((d_lru, C), BF16),
                        pltpu.VMEM((d_lru, C), F32), pltpu.VMEM((d_lru, C), F32),
                        pltpu.VMEM((2, d_s5 + d_lru, C), BF16)],
        compiler_params=pltpu.CompilerParams(dimension_semantics=("arbitrary", "arbitrary"),
                                             vmem_limit_bytes=48 * MIB),
        name="lru_out",
    )(zt, zt, zt, zt, zt, yt, cst, gw, wglu_t, wout_t)

    heads = 4
    M = mem_k.shape[1]
    tm = min(ATTN_ROWS, S)
    assert (tm // T) % SUBLANES == 0
    x2 = pl.pallas_call(
        functools.partial(_attn_kernel, heads=heads),
        grid=(B, S // tm),
        in_specs=[pl.BlockSpec((None, tm, D), lambda b, i: (b, i, 0)),
                  pl.BlockSpec((None, T, tm // T, D), lambda b, i: (b, 0, i, 0)),
                  _const_spec((1, D)), _const_spec((D, D)), _const_spec((D, D)),
                  pl.BlockSpec((None, M, D), lambda b, i: (b, 0, 0)),
                  pl.BlockSpec((None, M, D), lambda b, i: (b, 0, 0))],
        out_specs=pl.BlockSpec((None, tm, D), lambda b, i: (b, i, 0)),
        out_shape=jax.ShapeDtypeStruct((B, S, D), F32),
        compiler_params=pltpu.CompilerParams(dimension_semantics=("arbitrary", "arbitrary"),
                                             vmem_limit_bytes=48 * MIB),
        name="xattn",
    )(x, mix, p["ln_xa_g"].reshape(1, D), p["xa_w_q"].astype(BF16), p["xa_w_o"].astype(BF16), mem_k, mem_v)
    return x2


def _ffn_final(x, p, final_g):
    B, S, D = x.shape
    tm = min(FFN_ROWS, S)
    F2 = p["ffn_w_up"].shape[1]
    F = F2 // 2
    assert F % FFN_COLS == 0 and S % tm == 0
    cw = jnp.concatenate([p["ffn_conv_w"].astype(F32), p["ffn_conv_b"].astype(F32)[None, :]], axis=0)
    return pl.pallas_call(
        _ffn_kernel,
        grid=(B, S // tm),
        in_specs=[pl.BlockSpec((None, tm, D), lambda b, i: (b, i, 0)),
                  _const_spec((1, D)), _const_spec((1, D)), _const_spec((D, F2)),
                  _const_spec((4, F2)), _const_spec((F, D))],
        out_specs=pl.BlockSpec((None, tm, D), lambda b, i: (b, i, 0)),
        out_shape=jax.ShapeDtypeStruct((B, S, D), F32),
        scratch_shapes=[pltpu.VMEM((8, F2), F32), pltpu.VMEM((tm, F), BF16)],
        compiler_params=pltpu.CompilerParams(dimension_semantics=("arbitrary", "arbitrary"),
                                             vmem_limit_bytes=56 * MIB),
        name="conv_ffn",
    )(x, p["ln_ffn_g"].reshape(1, D), final_g.reshape(1, D), p["ffn_w_up"].astype(BF16), cw,
      p["ffn_w_down"].astype(BF16))


def kernel(x, mem, mem_norm_g, ln_mix_g, w_in, w_out, s5_lam_re, s5_lam_im, s5_log_dt, s5_b_re, s5_b_im, s5_c_re, s5_c_im, s5_d, s5_w_glu, s5_b_glu, lru_conv_w, lru_conv_b, lru_w_a, lru_b_a, lru_w_x, lru_b_x, lru_lam, ln_xa_g, xa_w_q, xa_w_k, xa_w_v, xa_w_o, ln_ffn_g, ffn_w_up, ffn_conv_w, ffn_conv_b, ffn_w_down, final_norm_g):
    params = dict(ln_mix_g=ln_mix_g, w_in=w_in, w_out=w_out, s5_lam_re=s5_lam_re, s5_lam_im=s5_lam_im,
                  s5_log_dt=s5_log_dt, s5_b_re=s5_b_re, s5_b_im=s5_b_im, s5_c_re=s5_c_re, s5_c_im=s5_c_im,
                  s5_d=s5_d, s5_w_glu=s5_w_glu, s5_b_glu=s5_b_glu, lru_conv_w=lru_conv_w,
                  lru_conv_b=lru_conv_b, lru_w_a=lru_w_a, lru_b_a=lru_b_a, lru_w_x=lru_w_x, lru_b_x=lru_b_x,
                  lru_lam=lru_lam, ln_xa_g=ln_xa_g, xa_w_q=xa_w_q, xa_w_k=xa_w_k, xa_w_v=xa_w_v,
                  xa_w_o=xa_w_o, ln_ffn_g=ln_ffn_g, ffn_w_up=ffn_w_up, ffn_conv_w=ffn_conv_w,
                  ffn_conv_b=ffn_conv_b, ffn_w_down=ffn_w_down)
    depth = ln_mix_g.shape[0]
    assert depth == 1, "the ConvFFN kernel fuses the final norm, which needs a single layer"
    B, M, D = mem.shape
    p = {k: v[0] for k, v in params.items()}
    mem_k, mem_v = pl.pallas_call(
        _mem_kv_kernel,
        grid=(B,),
        in_specs=[pl.BlockSpec((None, M, D), lambda b: (b, 0, 0)),
                  _const_spec((1, D)), _const_spec((D, D)), _const_spec((D, D))],
        out_specs=[pl.BlockSpec((None, M, D), lambda b: (b, 0, 0)),
                   pl.BlockSpec((None, M, D), lambda b: (b, 0, 0))],
        out_shape=[jax.ShapeDtypeStruct((B, M, D), BF16), jax.ShapeDtypeStruct((B, M, D), BF16)],
        compiler_params=pltpu.CompilerParams(dimension_semantics=("arbitrary",),
                                             vmem_limit_bytes=32 * MIB),
        name="mem_kv",
    )(mem, mem_norm_g.reshape(1, D), p["xa_w_k"].astype(BF16), p["xa_w_v"].astype(BF16))
    x2 = _layer(x, mem_k, mem_v, p)
    return _ffn_final(x2, p, final_norm_g)
```

```python
import functools
import math

import jax
import jax.numpy as jnp
from jax import lax
from jax.experimental import pallas as pl
from jax.experimental.pallas import tpu as pltpu

EPS = 1e-6
LRU_C = 8.0
S5_CHUNK = 16
FFN_ROWS = 1024
FFN_COLS = 256
ATTN_ROWS = 1024
S5_GROUP_BLOCK = 8
LRU_ROW_BLOCK = 32
LRU_POS = 4
SUBLANES = 8
LANES = 128
MIB = 1024 * 1024

F32 = jnp.float32
BF16 = jnp.bfloat16


def _gelu(x):
    half = 0.5 * x
    return half + half * jnp.tanh(x * (0.7978845608028654 + (0.7978845608028654 * 0.044715) * (x * x)))


def _sigmoid(x):
    return 0.5 * jnp.tanh(0.5 * x) + 0.5


def _roll_lanes(v, shift):
    C = v.shape[1]
    if shift % LANES == 0:
        return jnp.concatenate([v[:, C - shift:], v[:, :C - shift]], axis=1)
    return pltpu.roll(v, shift, 1)


def _rms_rows(x, g):
    ms = jnp.mean(x * x, axis=-1, keepdims=True)
    return x * lax.rsqrt(ms + EPS) * g


def _shift_lanes(v, lane, shift, fill):
    return jnp.where(lane >= shift, _roll_lanes(v, shift), fill)


def _nt_dot(a, b):
    return lax.dot_general(a, b, (((1,), (1,)), ((), ())), preferred_element_type=F32)


def _dot(a, b):
    return jnp.dot(a, b, preferred_element_type=F32)


def _mix_in_kernel(x_ref, g_ref, w_ref, zt_ref, hn_ref):
    q = pl.program_id(2)
    npos = zt_ref.shape[0]

    @pl.when(q == 0)
    def _first():
        xt = pltpu.einshape("cid->icd", x_ref[...])
        for i in range(SUBLANES):
            hn = _rms_rows(xt[i], g_ref[...]).astype(BF16)
            hn_ref[i] = hn
            if i < npos:
                zt_ref[i] = _nt_dot(w_ref[...], hn)

    @pl.when(q > 0)
    def _rest():
        for i in range(npos):
            zt_ref[i] = _nt_dot(w_ref[...], hn_ref[q * npos + i])


def _s5_kernel(zt_ref, toep_ref, bpow_ref, cpow_ref, lam_ref, d_ref, yt_ref, *, n_levels):
    T, rows, C = zt_ref.shape
    GB = S5_GROUP_BLOCK
    H = rows // GB
    P = bpow_ref.shape[1] // 2

    def group_input(gi):
        return zt_ref[:, gi * H:(gi + 1) * H, :].reshape(T * H, C)

    ends = [_dot(bpow_ref[gi], group_input(gi).astype(BF16)) for gi in range(GB)]
    hr = jnp.concatenate([e[:P] for e in ends], axis=0)
    hi = jnp.concatenate([e[P:] for e in ends], axis=0)
    for k in range(n_levels):
        mr, mi = lam_ref[2 * k], lam_ref[2 * k + 1]
        rr = _roll_lanes(hr, 1 << k)
        ri = _roll_lanes(hi, 1 << k)
        hr, hi = hr + mr * rr - mi * ri, hi + mr * ri + mi * rr
    lane = lax.broadcasted_iota(jnp.int32, (GB * P, C), 1)
    hr = _shift_lanes(hr, lane, 1, 0.0)
    hi = _shift_lanes(hi, lane, 1, 0.0)
    for gi in range(GB):
        u32 = group_input(gi)
        hst = jnp.concatenate([hr[gi * P:(gi + 1) * P], hi[gi * P:(gi + 1) * P]], axis=0).astype(BF16)
        y = _dot(toep_ref[gi], u32.astype(BF16)) + _dot(cpow_ref[gi], hst) + d_ref[gi] * u32
        yt_ref[:, gi * H:(gi + 1) * H, :] = _gelu(y).reshape(T, H, C)


def _lru_kernel(x_ref, g_ref, y5_ref, cst_ref, gw_ref, wglu_ref, wout_ref, out_ref,
                hloc_ref, acum_ref, hcur_ref, acur_ref, hin_ref, prev_ref,
                xc_ref, xb_ref, prea_ref, prex_ref, ycat_ref, *, T, n_levels):
    j = pl.program_id(1)
    PP = x_ref.shape[0]
    NS = T // PP
    N, C = hcur_ref.shape
    half = N // 2
    RB = LRU_ROW_BLOCK
    lane_rb = lax.broadcasted_iota(jnp.int32, (RB, C), 1)
    slabs = [slice(r0, r0 + RB) for r0 in range(0, N, RB)]

    def cst(k, r=slice(None)):
        return jnp.tile(cst_ref[k, r, :], (1, C // LANES))

    @pl.when(j == 0)
    def _halo():
        prev_ref[...] = x_ref[...]

    @pl.when((j >= 1) & (j <= NS))
    def _local_scan():
        m = j - 1

        @pl.when(m == 0)
        def _():
            hcur_ref[...] = jnp.zeros_like(hcur_ref)
            acur_ref[...] = jnp.ones_like(acur_ref)

        def conv(p, r, first_step):
            acc = cst(3, r) * x_ref[p, r, :] + cst(4, r)
            for k in range(1, 4):
                if p - k >= 0:
                    v = x_ref[p - k, r, :]
                else:
                    v = prev_ref[p - k + PP, r, :]
                    if first_step:
                        v = _shift_lanes(v, lane_rb, 1, 0.0)
                acc = acc + cst(3 - k, r) * v
            xc_ref[p % 2, r, :] = acc
            xb_ref[p % 2, r, :] = acc.astype(BF16)

        def scan(p, r):
            s = m * PP + p
            ra = _sigmoid(prea_ref[p % 2, r, :] + cst(5, r))
            ix = _sigmoid(prex_ref[p % 2, r, :] + cst(6, r))
            log_a = -(cst(7, r) * ra)
            a = jnp.exp(log_a)
            bx = jnp.sqrt(-jnp.tanh(log_a) * (a * a + 1.0)) * (ix * xc_ref[p % 2, r, :])
            h = a * hcur_ref[r, :] + bx
            acc_a = a * acur_ref[r, :]
            hcur_ref[r, :] = h
            acur_ref[r, :] = acc_a
            hloc_ref[s, r, :] = h
            acum_ref[s, r, :] = acc_a

        def positions(first_step):
            def run():
                def project_gates(p):
                    i = p % 2
                    for r in slabs:
                        conv(p, r, first_step)
                    prea_ref[i, :half] = _dot(gw_ref[0], xb_ref[i, :half])
                    prex_ref[i, :half] = _dot(gw_ref[2], xb_ref[i, :half])
                    prea_ref[i, half:] = _dot(gw_ref[1], xb_ref[i, half:])
                    prex_ref[i, half:] = _dot(gw_ref[3], xb_ref[i, half:])

                project_gates(0)
                for p in range(PP):
                    if p + 1 < PP:
                        project_gates(p + 1)
                    for r in slabs:
                        scan(p, r)
            return run

        lax.cond(m == 0, positions(True), positions(False))
        prev_ref[...] = x_ref[...]

        @pl.when(m == NS - 1)
        def _carry_scan():
            lane = lax.broadcasted_iota(jnp.int32, (N, C), 1)
            hc = hcur_ref[...]
            ac = acur_ref[...]
            for k in range(n_levels):
                sh = 1 << k
                hs = _shift_lanes(hc, lane, sh, 0.0)
                as_ = _shift_lanes(ac, lane, sh, 1.0)
                hc = ac * hs + hc
                ac = ac * as_
            hin_ref[...] = _shift_lanes(hc, lane, 1, 0.0)
            ycat_ref[1] = jnp.zeros(ycat_ref.shape[1:], BF16)

    @pl.when(j > NS)
    def _finish():
        k = j - NS - 1
        m = jnp.minimum(k, NS - 1)
        for p in range(PP):
            out_t = _dot(wout_ref[...], ycat_ref[(k + 1) % 2, p])
            out_ref[p] = out_t.T
            s = m * PP + p
            h = hloc_ref[s] + acum_ref[s] * hin_ref[...]
            y_lru = h * _gelu(g_ref[p])
            y5 = y5_ref[p]
            gate = _sigmoid(_dot(wglu_ref[...], y5.astype(BF16)) + cst(8))
            ycat_ref[k % 2, p] = jnp.concatenate([y5 * gate, y_lru], axis=0).astype(BF16)


def _mem_kv_kernel(mem_ref, g_ref, wk_ref, wv_ref, k_ref, v_ref):
    mn = _rms_rows(mem_ref[...], g_ref[...]).astype(BF16)
    k_ref[...] = _dot(mn, wk_ref[...]).astype(BF16)
    v_ref[...] = _dot(mn, wv_ref[...]).astype(BF16)


def _attn_kernel(x_ref, mix_ref, g_ref, wq_ref, wo_ref, k_ref, v_ref, out_ref, *, heads):
    mix = pltpu.einshape("scd->csd", mix_ref[...])
    x = x_ref[...] + mix.reshape(x_ref.shape)
    hn = _rms_rows(x, g_ref[...]).astype(BF16)
    q = _dot(hn, wq_ref[...])
    dh = q.shape[1] // heads
    scale = dh ** -0.5
    outs = []
    for h in range(heads):
        sl = slice(h * dh, (h + 1) * dh)
        sc = _nt_dot(q[:, sl].astype(BF16), k_ref[:, sl]) * scale
        p = jnp.exp(sc - jnp.max(sc, axis=-1, keepdims=True))
        denom = jnp.sum(p, axis=-1, keepdims=True)
        outs.append(_dot(p.astype(BF16), v_ref[:, sl]) / denom)
    o = jnp.concatenate(outs, axis=1).astype(BF16)
    out_ref[...] = x + _dot(o, wo_ref[...])


def _ffn_kernel(x_ref, gf_ref, gfin_ref, wup_ref, cw_ref, wdn_ref, out_ref, halo_ref, prod_ref):
    i = pl.program_id(1)
    M = x_ref.shape[0]
    F = prod_ref.shape[1]
    x = x_ref[...]
    hn = _rms_rows(x, gf_ref[...]).astype(BF16)

    @pl.when(i == 0)
    def _sequence_start():
        halo_ref[...] = jnp.zeros_like(halo_ref)

    row = lax.broadcasted_iota(jnp.int32, (8, FFN_COLS), 0)

    def conv(c0):
        cols = slice(c0, c0 + FFN_COLS)
        z = _dot(hn, wup_ref[:, cols])
        prev = halo_ref[:, cols]
        halo_ref[:, cols] = z[M - 8:]
        z1 = pltpu.roll(z, 1, 0)
        z2 = pltpu.roll(z, 2, 0)
        z1 = jnp.concatenate([jnp.where(row < 1, pltpu.roll(prev, 1, 0), z1[:8]), z1[8:]], axis=0)
        z2 = jnp.concatenate([jnp.where(row < 2, pltpu.roll(prev, 2, 0), z2[:8]), z2[8:]], axis=0)
        cw = cw_ref[:, cols]
        return cw[2:3] * z + cw[1:2] * z1 + cw[0:1] * z2 + cw[3:4]

    for f in range(F // FFN_COLS):
        val = conv(f * FFN_COLS)
        gate = conv(F + f * FFN_COLS)
        prod_ref[:, f * FFN_COLS:(f + 1) * FFN_COLS] = (_gelu(gate) * val).astype(BF16)
    y = x + _dot(prod_ref[...], wdn_ref[...])
    out_ref[...] = _rms_rows(y, gfin_ref[...])


def _s5_matrices(lam_re, lam_im, log_dt, b_re, b_im, c_re, c_im, T, n_levels, C):
    hp = lax.Precision.HIGHEST
    G, P = lam_re.shape
    H = b_re.shape[-1]
    lam = lax.complex(lam_re.astype(F32), lam_im.astype(F32))
    ldt = lam * jnp.exp(log_dt.astype(F32))[:, None]
    lam_bar = jnp.exp(ldt)
    b_bar = ((lam_bar - 1.0) / lam)[..., None] * lax.complex(b_re.astype(F32), b_im.astype(F32))
    lam_pow = jnp.exp(ldt[..., None] * jnp.arange(T + 1, dtype=F32))
    rep = jnp.repeat(jnp.eye(T, dtype=F32), H, axis=1)
    til = jnp.tile(jnp.eye(H, dtype=F32), (1, T))

    def expand(a, m, spec):
        return (jnp.einsum(spec, jnp.real(a), m, precision=hp), jnp.einsum(spec, jnp.imag(a), m, precision=hp))

    def cmul(a, b):
        return a[0] * b[0] - a[1] * b[1], a[0] * b[1] + a[1] * b[0]

    bb = expand(b_bar, til, "gpk,kc->gpc")
    lam_rev = expand(lam_pow[..., :T][..., ::-1], rep, "gpt,tc->gpc")
    lam_fwd = expand(lam_pow[..., :T], rep, "gpt,tc->gpc")
    bpow = jnp.concatenate(cmul(lam_rev, bb), axis=1)
    mk = cmul(lam_fwd, bb)
    kall = (jnp.einsum("ghp,gpc->ghc", c_re.astype(F32), mk[0], precision=hp)
            - jnp.einsum("ghp,gpc->ghc", c_im.astype(F32), mk[1], precision=hp))
    col = jnp.arange(T * H)
    shift = ((col[None, :, None] % H == col[None, None, :] % H)
             & (col[None, :, None] // H == jnp.arange(T)[:, None, None] - col[None, None, :] // H)).astype(F32)
    toep = jnp.einsum("ghj,tjc->gthc", kall, shift, precision=hp).reshape(G, T * H, T * H)
    lam_t = lam_pow[..., 1:].transpose(0, 2, 1)
    lam_rows = expand(lam_t, rep.T, "gtp,ct->gcp")
    c_rows = (jnp.einsum("ch,ghp->gcp", til.T, c_re.astype(F32), precision=hp),
              jnp.einsum("ch,ghp->gcp", til.T, c_im.astype(F32), precision=hp))
    cp = cmul(c_rows, lam_rows)
    cpow = jnp.concatenate([cp[0], -cp[1]], axis=2)
    shifts = 2 ** jnp.arange(n_levels)
    ls = jnp.exp(ldt[..., None] * (T * shifts).astype(F32))
    gb = S5_GROUP_BLOCK
    ls = jnp.stack([jnp.real(ls), jnp.imag(ls)], axis=-1)
    ls = ls.reshape(G // gb, gb * P, 2 * n_levels).transpose(0, 2, 1)
    mask = (jnp.arange(C)[None, :] >= jnp.repeat(shifts, 2)[:, None]).astype(F32)
    lamm = ls[..., None] * mask[None, :, None, :]
    return toep.astype(BF16), bpow.astype(BF16), cpow.astype(BF16), lamm


def _block_diag_t(w):
    nh, di, dj = w.shape
    eye = jnp.eye(nh, dtype=w.dtype)
    return jnp.einsum("hij,hg->hjgi", w, eye).reshape(nh * dj, nh * di)


def _const_spec(shape):
    return pl.BlockSpec(shape, lambda *_: (0,) * len(shape), pipeline_mode=pl.Buffered(1))


def _layer(x, mem_k, mem_v, p):
    B, S, D = x.shape
    T = S5_CHUNK
    C = S // T
    n_levels = int(math.log2(C))
    assert C == 1 << n_levels and S % T == 0
    d_s5 = p["s5_w_glu"].shape[0]
    d_lru = p["lru_conv_b"].shape[0]
    assert d_s5 == d_lru
    n_in = d_s5 + 2 * d_lru
    G = p["s5_lam_re"].shape[0]
    H = d_s5 // G
    assert G % S5_GROUP_BLOCK == 0

    assert T % SUBLANES == 0
    x4 = x.reshape(B, C, T, D)
    w_in_t = p["w_in"].T.astype(BF16)
    nsplit = 2
    npos = SUBLANES // nsplit
    zt = pl.pallas_call(
        _mix_in_kernel,
        grid=(B, T // SUBLANES, nsplit),
        in_specs=[pl.BlockSpec((None, C, SUBLANES, D), lambda b, h, q: (b, 0, h, 0)),
                  _const_spec((1, D)), _const_spec((n_in, D))],
        out_specs=pl.BlockSpec((None, npos, n_in, C), lambda b, h, q: (b, h * nsplit + q, 0, 0)),
        out_shape=jax.ShapeDtypeStruct((B, T, n_in, C), F32),
        scratch_shapes=[pltpu.VMEM((SUBLANES, C, D), BF16)],
        compiler_params=pltpu.CompilerParams(dimension_semantics=("arbitrary",) * 3,
                                             vmem_limit_bytes=56 * MIB),
        name="mix_in",
    )(x4, p["ln_mix_g"].reshape(1, D), w_in_t)

    toep, bpow, cpow, lamm = _s5_matrices(
        p["s5_lam_re"], p["s5_lam_im"], p["s5_log_dt"], p["s5_b_re"], p["s5_b_im"],
        p["s5_c_re"], p["s5_c_im"], T, n_levels, C)
    P2 = bpow.shape[1]
    d_rows = jnp.broadcast_to(jnp.tile(p["s5_d"].astype(F32), (1, T))[:, :, None], (G, T * H, C))
    gb = S5_GROUP_BLOCK
    yt = pl.pallas_call(
        functools.partial(_s5_kernel, n_levels=n_levels),
        grid=(G // gb, B),
        in_specs=[pl.BlockSpec((None, T, gb * H, C), lambda g, b: (b, 0, g, 0)),
                  pl.BlockSpec((gb, T * H, T * H), lambda g, b: (g, 0, 0)),
                  pl.BlockSpec((gb, P2, T * H), lambda g, b: (g, 0, 0)),
                  pl.BlockSpec((gb, T * H, P2), lambda g, b: (g, 0, 0)),
                  pl.BlockSpec((None, 2 * n_levels, gb * P2 // 2, C), lambda g, b: (g, 0, 0, 0)),
                  pl.BlockSpec((gb, T * H, C), lambda g, b: (g, 0, 0))],
        out_specs=pl.BlockSpec((None, T, gb * H, C), lambda g, b: (b, 0, g, 0)),
        out_shape=jax.ShapeDtypeStruct((B, T, d_s5, C), F32),
        compiler_params=pltpu.CompilerParams(dimension_semantics=("arbitrary", "arbitrary"),
                                             vmem_limit_bytes=48 * MIB),
        name="s5",
    )(zt, toep, bpow, cpow, lamm, d_rows)

    csp = LRU_C * jax.nn.softplus(-p["lru_lam"].astype(F32)).reshape(d_lru)
    cols = [p["lru_conv_w"][k] for k in range(4)] + [
        p["lru_conv_b"], p["lru_b_a"].reshape(d_lru), p["lru_b_x"].reshape(d_lru), csp, p["s5_b_glu"]]
    cst = jnp.broadcast_to(jnp.stack([c.astype(F32) for c in cols])[:, :, None], (len(cols), d_lru, LANES))
    assert C % LANES == 0
    half = d_lru // 2
    wa = _block_diag_t(p["lru_w_a"])
    wx = _block_diag_t(p["lru_w_x"])
    gw = jnp.stack([wa[:half, :half], wa[half:, half:], wx[:half, :half], wx[half:, half:]]).astype(BF16)
    wglu_t = p["s5_w_glu"].T.astype(BF16)
    wout_t = p["w_out"].T.astype(BF16)

    pp = LRU_POS
    ns = T // pp
    assert T % pp == 0 and pp >= 3

    def scanned(j):
        return jnp.where(j == 0, ns - 1, jnp.clip(j - 1, 0, ns - 1))

    def mixed(j):
        return jnp.clip(j - ns - 1, 0, ns - 1)

    def projected(j):
        return jnp.maximum(j - ns - 2, 0)

    mix = pl.pallas_call(
        functools.partial(_lru_kernel, T=T, n_levels=n_levels),
        grid=(B, 2 * ns + 2),
        in_specs=[pl.BlockSpec((None, pp, d_lru, C), lambda b, j: (b, scanned(j), 1, 0)),
                  pl.BlockSpec((None, pp, d_lru, C), lambda b, j: (b, mixed(j), 2, 0)),
                  pl.BlockSpec((None, pp, d_s5, C), lambda b, j: (b, mixed(j), 0, 0)),
                  _const_spec((len(cols), d_lru, LANES)), _const_spec((4, half, half)),
                  _const_spec((d_s5, d_s5)), _const_spec((D, d_s5 + d_lru))],
        out_specs=pl.BlockSpec((None, pp, C, D), lambda b, j: (b, projected(j), 0, 0)),
        out_shape=jax.ShapeDtypeStruct((B, T, C, D), F32),
        scratch_shapes=[pltpu.VMEM((T, d_lru, C), F32), pltpu.VMEM((T, d_lru, C), F32),
                        pltpu.VMEM((d_lru, C), F32), pltpu.VMEM((d_lru, C), F32),
                        pltpu.VMEM((d_lru, C), F32), pltpu.VMEM((pp, d_lru, C), F32),
                        pltpu.VMEM((2, d_lru, C), F32), pltpu.VMEM((2, d_lru, C), BF16),
                        pltpu.VMEM((2, d_lru, C), F32), pltpu.VMEM((2, d_lru, C), F32),
                        pltpu.VMEM((2, pp, d_s5 + d_lru, C), BF16)],
        compiler_params=pltpu.CompilerParams(dimension_semantics=("arbitrary", "arbitrary"),
                                             vmem_limit_bytes=60 * MIB),
        name="lru_out",
    )(zt, zt, yt, cst, gw, wglu_t, wout_t)

    heads = 4
    M = mem_k.shape[1]
    tm = min(ATTN_ROWS, S)
    assert (tm // T) % SUBLANES == 0
    x2 = pl.pallas_call(
        functools.partial(_attn_kernel, heads=heads),
        grid=(B, S // tm),
        in_specs=[pl.BlockSpec((None, tm, D), lambda b, i: (b, i, 0)),
                  pl.BlockSpec((None, T, tm // T, D), lambda b, i: (b, 0, i, 0)),
                  _const_spec((1, D)), _const_spec((D, D)), _const_spec((D, D)),
                  pl.BlockSpec((None, M, D), lambda b, i: (b, 0, 0)),
                  pl.BlockSpec((None, M, D), lambda b, i: (b, 0, 0))],
        out_specs=pl.BlockSpec((None, tm, D), lambda b, i: (b, i, 0)),
        out_shape=jax.ShapeDtypeStruct((B, S, D), F32),
        compiler_params=pltpu.CompilerParams(dimension_semantics=("arbitrary", "arbitrary"),
                                             vmem_limit_bytes=48 * MIB),
        name="xattn",
    )(x, mix, p["ln_xa_g"].reshape(1, D), p["xa_w_q"].astype(BF16), p["xa_w_o"].astype(BF16), mem_k, mem_v)
    return x2


def _ffn_final(x, p, final_g):
    B, S, D = x.shape
    tm = min(FFN_ROWS, S)
    F2 = p["ffn_w_up"].shape[1]
    F = F2 // 2
    assert F % FFN_COLS == 0 and S % tm == 0
    cw = jnp.concatenate([p["ffn_conv_w"].astype(F32), p["ffn_conv_b"].astype(F32)[None, :]], axis=0)
    return pl.pallas_call(
        _ffn_kernel,
        grid=(B, S // tm),
        in_specs=[pl.BlockSpec((None, tm, D), lambda b, i: (b, i, 0)),
                  _const_spec((1, D)), _const_spec((1, D)), _const_spec((D, F2)),
                  _const_spec((4, F2)), _const_spec((F, D))],
        out_specs=pl.BlockSpec((None, tm, D), lambda b, i: (b, i, 0)),
        out_shape=jax.ShapeDtypeStruct((B, S, D), F32),
        scratch_shapes=[pltpu.VMEM((8, F2), F32), pltpu.VMEM((tm, F), BF16)],
        compiler_params=pltpu.CompilerParams(dimension_semantics=("arbitrary", "arbitrary"),
                                             vmem_limit_bytes=56 * MIB),
        name="conv_ffn",
    )(x, p["ln_ffn_g"].reshape(1, D), final_g.reshape(1, D), p["ffn_w_up"].astype(BF16), cw,
      p["ffn_w_down"].astype(BF16))


def kernel(x, mem, mem_norm_g, ln_mix_g, w_in, w_out, s5_lam_re, s5_lam_im, s5_log_dt, s5_b_re, s5_b_im, s5_c_re, s5_c_im, s5_d, s5_w_glu, s5_b_glu, lru_conv_w, lru_conv_b, lru_w_a, lru_b_a, lru_w_x, lru_b_x, lru_lam, ln_xa_g, xa_w_q, xa_w_k, xa_w_v, xa_w_o, ln_ffn_g, ffn_w_up, ffn_conv_w, ffn_conv_b, ffn_w_down, final_norm_g):
    params = dict(ln_mix_g=ln_mix_g, w_in=w_in, w_out=w_out, s5_lam_re=s5_lam_re, s5_lam_im=s5_lam_im,
                  s5_log_dt=s5_log_dt, s5_b_re=s5_b_re, s5_b_im=s5_b_im, s5_c_re=s5_c_re, s5_c_im=s5_c_im,
                  s5_d=s5_d, s5_w_glu=s5_w_glu, s5_b_glu=s5_b_glu, lru_conv_w=lru_conv_w,
                  lru_conv_b=lru_conv_b, lru_w_a=lru_w_a, lru_b_a=lru_b_a, lru_w_x=lru_w_x, lru_b_x=lru_b_x,
                  lru_lam=lru_lam, ln_xa_g=ln_xa_g, xa_w_q=xa_w_q, xa_w_k=xa_w_k, xa_w_v=xa_w_v,
                  xa_w_o=xa_w_o, ln_ffn_g=ln_ffn_g, ffn_w_up=ffn_w_up, ffn_conv_w=ffn_conv_w,
                  ffn_conv_b=ffn_conv_b, ffn_w_down=ffn_w_down)
    depth = ln_mix_g.shape[0]
    assert depth == 1, "the ConvFFN kernel fuses the final norm, which needs a single layer"
    B, M, D = mem.shape
    p = {k: v[0] for k, v in params.items()}
    mem_k, mem_v = pl.pallas_call(
        _mem_kv_kernel,
        grid=(B,),
        in_specs=[pl.BlockSpec((None, M, D), lambda b: (b, 0, 0)),
                  _const_spec((1, D)), _const_spec((D, D)), _const_spec((D, D))],
        out_specs=[pl.BlockSpec((None, M, D), lambda b: (b, 0, 0)),
                   pl.BlockSpec((None, M, D), lambda b: (b, 0, 0))],
        out_shape=[jax.ShapeDtypeStruct((B, M, D), BF16), jax.ShapeDtypeStruct((B, M, D), BF16)],
        compiler_params=pltpu.CompilerParams(dimension_semantics=("arbitrary",),
                                             vmem_limit_bytes=32 * MIB),
        name="mem_kv",
    )(mem, mem_norm_g.reshape(1, D), p["xa_w_k"].astype(BF16), p["xa_w_v"].astype(BF16))
    x2 = _layer(x, mem_k, mem_v, p)
    return _ffn_final(x2, p, final_norm_g)
```

```python
import functools
import math

import jax
import jax.numpy as jnp
from jax import lax
from jax.experimental import pallas as pl
from jax.experimental.pallas import tpu as pltpu

EPS = 1e-6
LRU_C = 8.0
S5_CHUNK = 16
FFN_ROWS = 1024
FFN_COLS = 256
ATTN_ROWS = 1024
S5_GROUP_BLOCK = 8
LRU_ROW_BLOCK = 32
LRU_POS = 4
SUBLANES = 8
LANES = 128
MIB = 1024 * 1024

F32 = jnp.float32
BF16 = jnp.bfloat16


def _gelu(x):
    half = 0.5 * x
    return half + half * jnp.tanh(x * (0.7978845608028654 + (0.7978845608028654 * 0.044715) * (x * x)))


def _sigmoid(x):
    return 0.5 * jnp.tanh(0.5 * x) + 0.5


def _roll_lanes(v, shift):
    C = v.shape[1]
    if shift % LANES == 0:
        return jnp.concatenate([v[:, C - shift:], v[:, :C - shift]], axis=1)
    return pltpu.roll(v, shift, 1)


def _rms_rows(x, g):
    ms = jnp.mean(x * x, axis=-1, keepdims=True)
    return x * lax.rsqrt(ms + EPS) * g


def _shift_lanes(v, lane, shift, fill):
    return jnp.where(lane >= shift, _roll_lanes(v, shift), fill)


def _nt_dot(a, b):
    return lax.dot_general(a, b, (((1,), (1,)), ((), ())), preferred_element_type=F32)


def _dot(a, b):
    return jnp.dot(a, b, preferred_element_type=F32)


def _mix_in_kernel(x_ref, g_ref, w_ref, zt_ref):
    xt = pltpu.einshape("cid->icd", x_ref[...])
    for i in range(SUBLANES):
        hn = _rms_rows(xt[i], g_ref[...]).astype(BF16)
        zt_ref[i] = _nt_dot(w_ref[...], hn)


def _s5_kernel(zt_ref, toep_ref, bpow_ref, cpow_ref, lam_ref, d_ref, yt_ref, *, n_levels):
    T, rows, C = zt_ref.shape
    GB = S5_GROUP_BLOCK
    H = rows // GB
    P = bpow_ref.shape[1] // 2

    def group_input(gi):
        return zt_ref[:, gi * H:(gi + 1) * H, :].reshape(T * H, C)

    ends = [_dot(bpow_ref[gi], group_input(gi).astype(BF16)) for gi in range(GB)]
    hr = jnp.concatenate([e[:P] for e in ends], axis=0)
    hi = jnp.concatenate([e[P:] for e in ends], axis=0)
    for k in range(n_levels):
        mr, mi = lam_ref[2 * k], lam_ref[2 * k + 1]
        rr = _roll_lanes(hr, 1 << k)
        ri = _roll_lanes(hi, 1 << k)
        hr, hi = hr + mr * rr - mi * ri, hi + mr * ri + mi * rr
    lane = lax.broadcasted_iota(jnp.int32, (GB * P, C), 1)
    hr = _shift_lanes(hr, lane, 1, 0.0)
    hi = _shift_lanes(hi, lane, 1, 0.0)
    for gi in range(GB):
        u32 = group_input(gi)
        hst = jnp.concatenate([hr[gi * P:(gi + 1) * P], hi[gi * P:(gi + 1) * P]], axis=0).astype(BF16)
        y = _dot(toep_ref[gi], u32.astype(BF16)) + _dot(cpow_ref[gi], hst) + d_ref[gi] * u32
        yt_ref[:, gi * H:(gi + 1) * H, :] = _gelu(y).reshape(T, H, C)


def _lru_kernel(x_ref, g_ref, y5_ref, cst_ref, gw_ref, wglu_ref, wout_ref, out_ref,
                hloc_ref, acum_ref, hcur_ref, acur_ref, hin_ref, prev_ref,
                xc_ref, xb_ref, prea_ref, prex_ref, ycat_ref, *, T, n_levels):
    j = pl.program_id(1)
    PP = x_ref.shape[0]
    NS = T // PP
    N, C = hcur_ref.shape
    half = N // 2
    RB = LRU_ROW_BLOCK
    lane_rb = lax.broadcasted_iota(jnp.int32, (RB, C), 1)
    slabs = [slice(r0, r0 + RB) for r0 in range(0, N, RB)]

    def cst(k, r=slice(None)):
        return jnp.tile(cst_ref[k, r, :], (1, C // LANES))

    @pl.when(j == 0)
    def _halo():
        prev_ref[...] = x_ref[...]

    @pl.when((j >= 1) & (j <= NS))
    def _local_scan():
        m = j - 1

        @pl.when(m == 0)
        def _():
            hcur_ref[...] = jnp.zeros_like(hcur_ref)
            acur_ref[...] = jnp.ones_like(acur_ref)

        def conv(p, r, first_step):
            acc = cst(3, r) * x_ref[p, r, :] + cst(4, r)
            for k in range(1, 4):
                if p - k >= 0:
                    v = x_ref[p - k, r, :]
                else:
                    v = prev_ref[p - k + PP, r, :]
                    if first_step:
                        v = _shift_lanes(v, lane_rb, 1, 0.0)
                acc = acc + cst(3 - k, r) * v
            xc_ref[p % 2, r, :] = acc
            xb_ref[p % 2, r, :] = acc.astype(BF16)

        def scan(p, r):
            s = m * PP + p
            ra = _sigmoid(prea_ref[p % 2, r, :] + cst(5, r))
            ix = _sigmoid(prex_ref[p % 2, r, :] + cst(6, r))
            log_a = -(cst(7, r) * ra)
            a = jnp.exp(log_a)
            bx = jnp.sqrt(-jnp.tanh(log_a) * (a * a + 1.0)) * (ix * xc_ref[p % 2, r, :])
            h = a * hcur_ref[r, :] + bx
            acc_a = a * acur_ref[r, :]
            hcur_ref[r, :] = h
            acur_ref[r, :] = acc_a
            hloc_ref[s, r, :] = h
            acum_ref[s, r, :] = acc_a

        def positions(first_step):
            def run():
                def project_gates(p):
                    i = p % 2
                    for r in slabs:
                        conv(p, r, first_step)
                    prea_ref[i, :half] = _dot(gw_ref[0], xb_ref[i, :half])
                    prex_ref[i, :half] = _dot(gw_ref[2], xb_ref[i, :half])
                    prea_ref[i, half:] = _dot(gw_ref[1], xb_ref[i, half:])
                    prex_ref[i, half:] = _dot(gw_ref[3], xb_ref[i, half:])

                project_gates(0)
                for p in range(PP):
                    if p + 1 < PP:
                        project_gates(p + 1)
                    for r in slabs:
                        scan(p, r)
            return run

        lax.cond(m == 0, positions(True), positions(False))
        prev_ref[...] = x_ref[...]

        @pl.when(m == NS - 1)
        def _carry_scan():
            lane = lax.broadcasted_iota(jnp.int32, (N, C), 1)
            hc = hcur_ref[...]
            ac = acur_ref[...]
            for k in range(n_levels):
                sh = 1 << k
                hs = _shift_lanes(hc, lane, sh, 0.0)
                as_ = _shift_lanes(ac, lane, sh, 1.0)
                hc = ac * hs + hc
                ac = ac * as_
            hin_ref[...] = _shift_lanes(hc, lane, 1, 0.0)

    @pl.when(j > NS)
    def _finish():
        k = j - NS - 1

        def project(p):
            out_t = _dot(wout_ref[...], ycat_ref[(k + 1) % 2, p])
            out_ref[p] = out_t.T

        def mix(p):
            s = k * PP + p
            h = hloc_ref[s] + acum_ref[s] * hin_ref[...]
            y_lru = h * _gelu(g_ref[p])
            y5 = y5_ref[p]
            gate = _sigmoid(_dot(wglu_ref[...], y5.astype(BF16)) + cst(8))
            ycat_ref[k % 2, p] = jnp.concatenate([y5 * gate, y_lru], axis=0).astype(BF16)

        @pl.when(k == 0)
        def _():
            for p in range(PP):
                mix(p)

        @pl.when((k > 0) & (k < NS))
        def _():
            for p in range(PP):
                project(p)
                mix(p)

        @pl.when(k == NS)
        def _():
            for p in range(PP):
                project(p)


def _mem_kv_kernel(mem_ref, g_ref, wk_ref, wv_ref, k_ref, v_ref):
    mn = _rms_rows(mem_ref[...], g_ref[...]).astype(BF16)
    k_ref[...] = _dot(mn, wk_ref[...]).astype(BF16)
    v_ref[...] = _dot(mn, wv_ref[...]).astype(BF16)


def _attn_kernel(x_ref, mix_ref, g_ref, wq_ref, wo_ref, k_ref, v_ref, out_ref, *, heads):
    mix = pltpu.einshape("scd->csd", mix_ref[...])
    x = x_ref[...] + mix.reshape(x_ref.shape)
    hn = _rms_rows(x, g_ref[...]).astype(BF16)
    q = _dot(hn, wq_ref[...])
    dh = q.shape[1] // heads
    scale = dh ** -0.5
    outs = []
    for h in range(heads):
        sl = slice(h * dh, (h + 1) * dh)
        sc = _nt_dot(q[:, sl].astype(BF16), k_ref[:, sl]) * scale
        p = jnp.exp(sc - jnp.max(sc, axis=-1, keepdims=True))
        denom = jnp.sum(p, axis=-1, keepdims=True)
        outs.append(_dot(p.astype(BF16), v_ref[:, sl]) / denom)
    o = jnp.concatenate(outs, axis=1).astype(BF16)
    out_ref[...] = x + _dot(o, wo_ref[...])


def _ffn_kernel(x_ref, gf_ref, gfin_ref, wup_ref, cw_ref, wdn_ref, out_ref, halo_ref, prod_ref):
    i = pl.program_id(1)
    M = x_ref.shape[0]
    F = prod_ref.shape[1]
    x = x_ref[...]
    hn = _rms_rows(x, gf_ref[...]).astype(BF16)

    @pl.when(i == 0)
    def _sequence_start():
        halo_ref[...] = jnp.zeros_like(halo_ref)

    row = lax.broadcasted_iota(jnp.int32, (8, FFN_COLS), 0)

    def conv(c0):
        cols = slice(c0, c0 + FFN_COLS)
        z = _dot(hn, wup_ref[:, cols])
        prev = halo_ref[:, cols]
        halo_ref[:, cols] = z[M - 8:]
        z1 = pltpu.roll(z, 1, 0)
        z2 = pltpu.roll(z, 2, 0)
        z1 = jnp.concatenate([jnp.where(row < 1, pltpu.roll(prev, 1, 0), z1[:8]), z1[8:]], axis=0)
        z2 = jnp.concatenate([jnp.where(row < 2, pltpu.roll(prev, 2, 0), z2[:8]), z2[8:]], axis=0)
        cw = cw_ref[:, cols]
        return cw[2:3] * z + cw[1:2] * z1 + cw[0:1] * z2 + cw[3:4]

    for f in range(F // FFN_COLS):
        val = conv(f * FFN_COLS)
        gate = conv(F + f * FFN_COLS)
        prod_ref[:, f * FFN_COLS:(f + 1) * FFN_COLS] = (_gelu(gate) * val).astype(BF16)
    y = x + _dot(prod_ref[...], wdn_ref[...])
    out_ref[...] = _rms_rows(y, gfin_ref[...])


def _s5_matrices(lam_re, lam_im, log_dt, b_re, b_im, c_re, c_im, T, n_levels, C):
    hp = lax.Precision.HIGHEST
    G, P = lam_re.shape
    H = b_re.shape[-1]
    lam = lax.complex(lam_re.astype(F32), lam_im.astype(F32))
    ldt = lam * jnp.exp(log_dt.astype(F32))[:, None]
    lam_bar = jnp.exp(ldt)
    b_bar = ((lam_bar - 1.0) / lam)[..., None] * lax.complex(b_re.astype(F32), b_im.astype(F32))
    lam_pow = jnp.exp(ldt[..., None] * jnp.arange(T + 1, dtype=F32))
    rep = jnp.repeat(jnp.eye(T, dtype=F32), H, axis=1)
    til = jnp.tile(jnp.eye(H, dtype=F32), (1, T))

    def expand(a, m, spec):
        return (jnp.einsum(spec, jnp.real(a), m, precision=hp), jnp.einsum(spec, jnp.imag(a), m, precision=hp))

    def cmul(a, b):
        return a[0] * b[0] - a[1] * b[1], a[0] * b[1] + a[1] * b[0]

    bb = expand(b_bar, til, "gpk,kc->gpc")
    lam_rev = expand(lam_pow[..., :T][..., ::-1], rep, "gpt,tc->gpc")
    lam_fwd = expand(lam_pow[..., :T], rep, "gpt,tc->gpc")
    bpow = jnp.concatenate(cmul(lam_rev, bb), axis=1)
    mk = cmul(lam_fwd, bb)
    kall = (jnp.einsum("ghp,gpc->ghc", c_re.astype(F32), mk[0], precision=hp)
            - jnp.einsum("ghp,gpc->ghc", c_im.astype(F32), mk[1], precision=hp))
    col = jnp.arange(T * H)
    shift = ((col[None, :, None] % H == col[None, None, :] % H)
             & (col[None, :, None] // H == jnp.arange(T)[:, None, None] - col[None, None, :] // H)).astype(F32)
    toep = jnp.einsum("ghj,tjc->gthc", kall, shift, precision=hp).reshape(G, T * H, T * H)
    lam_t = lam_pow[..., 1:].transpose(0, 2, 1)
    lam_rows = expand(lam_t, rep.T, "gtp,ct->gcp")
    c_rows = (jnp.einsum("ch,ghp->gcp", til.T, c_re.astype(F32), precision=hp),
              jnp.einsum("ch,ghp->gcp", til.T, c_im.astype(F32), precision=hp))
    cp = cmul(c_rows, lam_rows)
    cpow = jnp.concatenate([cp[0], -cp[1]], axis=2)
    shifts = 2 ** jnp.arange(n_levels)
    ls = jnp.exp(ldt[..., None] * (T * shifts).astype(F32))
    gb = S5_GROUP_BLOCK
    ls = jnp.stack([jnp.real(ls), jnp.imag(ls)], axis=-1)
    ls = ls.reshape(G // gb, gb * P, 2 * n_levels).transpose(0, 2, 1)
    mask = (jnp.arange(C)[None, :] >= jnp.repeat(shifts, 2)[:, None]).astype(F32)
    lamm = ls[..., None] * mask[None, :, None, :]
    return toep.astype(BF16), bpow.astype(BF16), cpow.astype(BF16), lamm


def _block_diag_t(w):
    nh, di, dj = w.shape
    eye = jnp.eye(nh, dtype=w.dtype)
    return jnp.einsum("hij,hg->hjgi", w, eye).reshape(nh * dj, nh * di)


def _const_spec(shape):
    return pl.BlockSpec(shape, lambda *_: (0,) * len(shape), pipeline_mode=pl.Buffered(1))


def _layer(x, mem_k, mem_v, p):
    B, S, D = x.shape
    T = S5_CHUNK
    C = S // T
    n_levels = int(math.log2(C))
    assert C == 1 << n_levels and S % T == 0
    d_s5 = p["s5_w_glu"].shape[0]
    d_lru = p["lru_conv_b"].shape[0]
    assert d_s5 == d_lru
    n_in = d_s5 + 2 * d_lru
    G = p["s5_lam_re"].shape[0]
    H = d_s5 // G
    assert G % S5_GROUP_BLOCK == 0

    assert T % SUBLANES == 0
    x4 = x.reshape(B, C, T, D)
    w_in_t = p["w_in"].T.astype(BF16)
    zt = pl.pallas_call(
        _mix_in_kernel,
        grid=(B, T // SUBLANES),
        in_specs=[pl.BlockSpec((None, C, SUBLANES, D), lambda b, h: (b, 0, h, 0)),
                  _const_spec((1, D)), _const_spec((n_in, D))],
        out_specs=pl.BlockSpec((None, SUBLANES, n_in, C), lambda b, h: (b, h, 0, 0)),
        out_shape=jax.ShapeDtypeStruct((B, T, n_in, C), F32),
        compiler_params=pltpu.CompilerParams(dimension_semantics=("arbitrary",) * 2,
                                             vmem_limit_bytes=62 * MIB),
        name="mix_in",
    )(x4, p["ln_mix_g"].reshape(1, D), w_in_t)

    toep, bpow, cpow, lamm = _s5_matrices(
        p["s5_lam_re"], p["s5_lam_im"], p["s5_log_dt"], p["s5_b_re"], p["s5_b_im"],
        p["s5_c_re"], p["s5_c_im"], T, n_levels, C)
    P2 = bpow.shape[1]
    d_rows = jnp.broadcast_to(jnp.tile(p["s5_d"].astype(F32), (1, T))[:, :, None], (G, T * H, C))
    gb = S5_GROUP_BLOCK
    yt = pl.pallas_call(
        functools.partial(_s5_kernel, n_levels=n_levels),
        grid=(G // gb, B),
        in_specs=[pl.BlockSpec((None, T, gb * H, C), lambda g, b: (b, 0, g, 0)),
                  pl.BlockSpec((gb, T * H, T * H), lambda g, b: (g, 0, 0)),
                  pl.BlockSpec((gb, P2, T * H), lambda g, b: (g, 0, 0)),
                  pl.BlockSpec((gb, T * H, P2), lambda g, b: (g, 0, 0)),
                  pl.BlockSpec((None, 2 * n_levels, gb * P2 // 2, C), lambda g, b: (g, 0, 0, 0)),
                  pl.BlockSpec((gb, T * H, C), lambda g, b: (g, 0, 0))],
        out_specs=pl.BlockSpec((None, T, gb * H, C), lambda g, b: (b, 0, g, 0)),
        out_shape=jax.ShapeDtypeStruct((B, T, d_s5, C), F32),
        compiler_params=pltpu.CompilerParams(dimension_semantics=("arbitrary", "arbitrary"),
                                             vmem_limit_bytes=48 * MIB),
        name="s5",
    )(zt, toep, bpow, cpow, lamm, d_rows)

    csp = LRU_C * jax.nn.softplus(-p["lru_lam"].astype(F32)).reshape(d_lru)
    cols = [p["lru_conv_w"][k] for k in range(4)] + [
        p["lru_conv_b"], p["lru_b_a"].reshape(d_lru), p["lru_b_x"].reshape(d_lru), csp, p["s5_b_glu"]]
    cst = jnp.broadcast_to(jnp.stack([c.astype(F32) for c in cols])[:, :, None], (len(cols), d_lru, LANES))
    assert C % LANES == 0
    half = d_lru // 2
    wa = _block_diag_t(p["lru_w_a"])
    wx = _block_diag_t(p["lru_w_x"])
    gw = jnp.stack([wa[:half, :half], wa[half:, half:], wx[:half, :half], wx[half:, half:]]).astype(BF16)
    wglu_t = p["s5_w_glu"].T.astype(BF16)
    wout_t = p["w_out"].T.astype(BF16)

    pp = LRU_POS
    ns = T // pp
    assert T % pp == 0 and pp >= 3

    def scanned(j):
        return jnp.where(j == 0, ns - 1, jnp.clip(j - 1, 0, ns - 1))

    def mixed(j):
        return jnp.clip(j - ns - 1, 0, ns - 1)

    def projected(j):
        return jnp.maximum(j - ns - 2, 0)

    mix = pl.pallas_call(
        functools.partial(_lru_kernel, T=T, n_levels=n_levels),
        grid=(B, 2 * ns + 2),
        in_specs=[pl.BlockSpec((None, pp, d_lru, C), lambda b, j: (b, scanned(j), 1, 0)),
                  pl.BlockSpec((None, pp, d_lru, C), lambda b, j: (b, mixed(j), 2, 0)),
                  pl.BlockSpec((None, pp, d_s5, C), lambda b, j: (b, mixed(j), 0, 0)),
                  _const_spec((len(cols), d_lru, LANES)), _const_spec((4, half, half)),
                  _const_spec((d_s5, d_s5)), _const_spec((D, d_s5 + d_lru))],
        out_specs=pl.BlockSpec((None, pp, C, D), lambda b, j: (b, projected(j), 0, 0)),
        out_shape=jax.ShapeDtypeStruct((B, T, C, D), F32),
        scratch_shapes=[pltpu.VMEM((T, d_lru, C), F32), pltpu.VMEM((T, d_lru, C), F32),
                        pltpu.VMEM((d_lru, C), F32), pltpu.VMEM((d_lru, C), F32),
                        pltpu.VMEM((d_lru, C), F32), pltpu.VMEM((pp, d_lru, C), F32),
                        pltpu.VMEM((2, d_lru, C), F32), pltpu.VMEM((2, d_lru, C), BF16),
                        pltpu.VMEM((2, d_lru, C), F32), pltpu.VMEM((2, d_lru, C), F32),
                        pltpu.VMEM((2, pp, d_s5 + d_lru, C), BF16)],
        compiler_params=pltpu.CompilerParams(dimension_semantics=("arbitrary", "arbitrary"),
                                             vmem_limit_bytes=60 * MIB),
        name="lru_out",
    )(zt, zt, yt, cst, gw, wglu_t, wout_t)

    heads = 4
    M = mem_k.shape[1]
    tm = min(ATTN_ROWS, S)
    assert (tm // T) % SUBLANES == 0
    x2 = pl.pallas_call(
        functools.partial(_attn_kernel, heads=heads),
        grid=(B, S // tm),
        in_specs=[pl.BlockSpec((None, tm, D), lambda b, i: (b, i, 0)),
                  pl.BlockSpec((None, T, tm // T, D), lambda b, i: (b, 0, i, 0)),
                  _const_spec((1, D)), _const_spec((D, D)), _const_spec((D, D)),
                  pl.BlockSpec((None, M, D), lambda b, i: (b, 0, 0)),
                  pl.BlockSpec((None, M, D), lambda b, i: (b, 0, 0))],
        out_specs=pl.BlockSpec((None, tm, D), lambda b, i: (b, i, 0)),
        out_shape=jax.ShapeDtypeStruct((B, S, D), F32),
        compiler_params=pltpu.CompilerParams(dimension_semantics=("arbitrary", "arbitrary"),
                                             vmem_limit_bytes=48 * MIB),
        name="xattn",
    )(x, mix, p["ln_xa_g"].reshape(1, D), p["xa_w_q"].astype(BF16), p["xa_w_o"].astype(BF16), mem_k, mem_v)
    return x2


def _ffn_final(x, p, final_g):
    B, S, D = x.shape
    tm = min(FFN_ROWS, S)
    F2 = p["ffn_w_up"].shape[1]
    F = F2 // 2
    assert F % FFN_COLS == 0 and S % tm == 0
    cw = jnp.concatenate([p["ffn_conv_w"].astype(F32), p["ffn_conv_b"].astype(F32)[None, :]], axis=0)
    return pl.pallas_call(
        _ffn_kernel,
        grid=(B, S // tm),
        in_specs=[pl.BlockSpec((None, tm, D), lambda b, i: (b, i, 0)),
                  _const_spec((1, D)), _const_spec((1, D)), _const_spec((D, F2)),
                  _const_spec((4, F2)), _const_spec((F, D))],
        out_specs=pl.BlockSpec((None, tm, D), lambda b, i: (b, i, 0)),
        out_shape=jax.ShapeDtypeStruct((B, S, D), F32),
        scratch_shapes=[pltpu.VMEM((8, F2), F32), pltpu.VMEM((tm, F), BF16)],
        compiler_params=pltpu.CompilerParams(dimension_semantics=("arbitrary", "arbitrary"),
                                             vmem_limit_bytes=56 * MIB),
        name="conv_ffn",
    )(x, p["ln_ffn_g"].reshape(1, D), final_g.reshape(1, D), p["ffn_w_up"].astype(BF16), cw,
      p["ffn_w_down"].astype(BF16))


def kernel(x, mem, mem_norm_g, ln_mix_g, w_in, w_out, s5_lam_re, s5_lam_im, s5_log_dt, s5_b_re, s5_b_im, s5_c_re, s5_c_im, s5_d, s5_w_glu, s5_b_glu, lru_conv_w, lru_conv_b, lru_w_a, lru_b_a, lru_w_x, lru_b_x, lru_lam, ln_xa_g, xa_w_q, xa_w_k, xa_w_v, xa_w_o, ln_ffn_g, ffn_w_up, ffn_conv_w, ffn_conv_b, ffn_w_down, final_norm_g):
    params = dict(ln_mix_g=ln_mix_g, w_in=w_in, w_out=w_out, s5_lam_re=s5_lam_re, s5_lam_im=s5_lam_im,
                  s5_log_dt=s5_log_dt, s5_b_re=s5_b_re, s5_b_im=s5_b_im, s5_c_re=s5_c_re, s5_c_im=s5_c_im,
                  s5_d=s5_d, s5_w_glu=s5_w_glu, s5_b_glu=s5_b_glu, lru_conv_w=lru_conv_w,
                  lru_conv_b=lru_conv_b, lru_w_a=lru_w_a, lru_b_a=lru_b_a, lru_w_x=lru_w_x, lru_b_x=lru_b_x,
                  lru_lam=lru_lam, ln_xa_g=ln_xa_g, xa_w_q=xa_w_q, xa_w_k=xa_w_k, xa_w_v=xa_w_v,
                  xa_w_o=xa_w_o, ln_ffn_g=ln_ffn_g, ffn_w_up=ffn_w_up, ffn_conv_w=ffn_conv_w,
                  ffn_conv_b=ffn_conv_b, ffn_w_down=ffn_w_down)
    depth = ln_mix_g.shape[0]
    assert depth == 1, "the ConvFFN kernel fuses the final norm, which needs a single layer"
    B, M, D = mem.shape
    p = {k: v[0] for k, v in params.items()}
    mem_k, mem_v = pl.pallas_call(
        _mem_kv_kernel,
        grid=(B,),
        in_specs=[pl.BlockSpec((None, M, D), lambda b: (b, 0, 0)),
                  _const_spec((1, D)), _const_spec((D, D)), _const_spec((D, D))],
        out_specs=[pl.BlockSpec((None, M, D), lambda b: (b, 0, 0)),
                   pl.BlockSpec((None, M, D), lambda b: (b, 0, 0))],
        out_shape=[jax.ShapeDtypeStruct((B, M, D), BF16), jax.ShapeDtypeStruct((B, M, D), BF16)],
        compiler_params=pltpu.CompilerParams(dimension_semantics=("arbitrary",),
                                             vmem_limit_bytes=32 * MIB),
        name="mem_kv",
    )(mem, mem_norm_g.reshape(1, D), p["xa_w_k"].astype(BF16), p["xa_w_v"].astype(BF16))
    x2 = _layer(x, mem_k, mem_v, p)
    return _ffn_final(x2, p, final_norm_g)
```

```python
import functools
import math

import jax
import jax.numpy as jnp
from jax import lax
from jax.experimental import pallas as pl
from jax.experimental.pallas import tpu as pltpu

EPS = 1e-6
LRU_C = 8.0
S5_CHUNK = 16
FFN_ROWS = 1024
FFN_COLS = 256
ATTN_ROWS = 1024
MEM_ROWS = 1024
S5_GROUP_BLOCK = 8
LRU_ROW_BLOCK = 32
LRU_POS = 4
SUBLANES = 8
LANES = 128
MIB = 1024 * 1024

F32 = jnp.float32
BF16 = jnp.bfloat16


def _gelu(x):
    half = 0.5 * x
    return half + half * jnp.tanh(x * (0.7978845608028654 + (0.7978845608028654 * 0.044715) * (x * x)))


def _sigmoid(x):
    return 0.5 * jnp.tanh(0.5 * x) + 0.5


def _roll_lanes(v, shift):
    C = v.shape[1]
    if shift % LANES == 0:
        return jnp.concatenate([v[:, C - shift:], v[:, :C - shift]], axis=1)
    return pltpu.roll(v, shift, 1)


def _rms_rows(x, g):
    ms = jnp.mean(x * x, axis=-1, keepdims=True)
    return x * lax.rsqrt(ms + EPS) * g


def _shift_lanes(v, lane, shift, fill):
    return jnp.where(lane >= shift, _roll_lanes(v, shift), fill)


def _nt_dot(a, b):
    return lax.dot_general(a, b, (((1,), (1,)), ((), ())), preferred_element_type=F32)


def _dot(a, b):
    return jnp.dot(a, b, preferred_element_type=F32)


def _mix_in_kernel(x_ref, g_ref, w_ref, zt_ref):
    xt = pltpu.einshape("cid->icd", x_ref[...])
    for i in range(SUBLANES):
        hn = _rms_rows(xt[i], g_ref[...]).astype(BF16)
        zt_ref[i] = _nt_dot(w_ref[...], hn)


def _s5_kernel(zt_ref, toep_ref, bpow_ref, cpow_ref, lam_ref, d_ref, yt_ref, *, n_levels):
    T, rows, C = zt_ref.shape
    GB = S5_GROUP_BLOCK
    H = rows // GB
    P = bpow_ref.shape[1] // 2

    def group_input(gi):
        return zt_ref[:, gi * H:(gi + 1) * H, :].reshape(T * H, C)

    ends = [_dot(bpow_ref[gi], group_input(gi).astype(BF16)) for gi in range(GB)]
    hr = jnp.concatenate([e[:P] for e in ends], axis=0)
    hi = jnp.concatenate([e[P:] for e in ends], axis=0)
    for k in range(n_levels):
        mr, mi = lam_ref[2 * k], lam_ref[2 * k + 1]
        rr = _roll_lanes(hr, 1 << k)
        ri = _roll_lanes(hi, 1 << k)
        hr, hi = hr + mr * rr - mi * ri, hi + mr * ri + mi * rr
    lane = lax.broadcasted_iota(jnp.int32, (GB * P, C), 1)
    hr = _shift_lanes(hr, lane, 1, 0.0)
    hi = _shift_lanes(hi, lane, 1, 0.0)
    for gi in range(GB):
        u32 = group_input(gi)
        hst = jnp.concatenate([hr[gi * P:(gi + 1) * P], hi[gi * P:(gi + 1) * P]], axis=0).astype(BF16)
        skip = jnp.tile(d_ref[gi], (1, C // LANES)) * u32
        y = _dot(toep_ref[gi], u32.astype(BF16)) + _dot(cpow_ref[gi], hst) + skip
        yt_ref[:, gi * H:(gi + 1) * H, :] = _gelu(y).reshape(T, H, C)


def _lru_kernel(x_ref, g_ref, y5_ref, cst_ref, gw_ref, wglu_ref, wout_ref, out_ref,
                hloc_ref, acum_ref, hcur_ref, acur_ref, hin_ref, prev_ref,
                xc_ref, xb_ref, prea_ref, prex_ref, ycat_ref, *, T, n_levels):
    j = pl.program_id(1)
    PP = x_ref.shape[0]
    NS = T // PP
    N, C = hcur_ref.shape
    half = N // 2
    RB = LRU_ROW_BLOCK
    lane_rb = lax.broadcasted_iota(jnp.int32, (RB, C), 1)
    slabs = [slice(r0, r0 + RB) for r0 in range(0, N, RB)]

    def cst(k, r=slice(None)):
        return jnp.tile(cst_ref[k, r, :], (1, C // LANES))

    @pl.when(j == 0)
    def _halo():
        prev_ref[...] = x_ref[...]

    @pl.when((j >= 1) & (j <= NS))
    def _local_scan():
        m = j - 1

        @pl.when(m == 0)
        def _():
            hcur_ref[...] = jnp.zeros_like(hcur_ref)
            acur_ref[...] = jnp.ones_like(acur_ref)

        def conv(p, r, first_step):
            acc = cst(3, r) * x_ref[p, r, :] + cst(4, r)
            for k in range(1, 4):
                if p - k >= 0:
                    v = x_ref[p - k, r, :]
                else:
                    v = prev_ref[p - k + PP, r, :]
                    if first_step:
                        v = _shift_lanes(v, lane_rb, 1, 0.0)
                acc = acc + cst(3 - k, r) * v
            xc_ref[p % 2, r, :] = acc
            xb_ref[p % 2, r, :] = acc.astype(BF16)

        def scan(p, r):
            s = m * PP + p
            ta = jnp.tanh(prea_ref[p % 2, r, :] + cst(5, r))
            tx = jnp.tanh(prex_ref[p % 2, r, :] + cst(6, r))
            k7 = cst(7, r)
            log_a = k7 * ta + k7
            a = jnp.exp(log_a)
            hx = 0.5 * xc_ref[p % 2, r, :]
            bx = jnp.sqrt(-jnp.tanh(log_a) * (a * a + 1.0)) * (hx * tx + hx)
            h = a * hcur_ref[r, :] + bx
            acc_a = a * acur_ref[r, :]
            hcur_ref[r, :] = h
            acur_ref[r, :] = acc_a
            hloc_ref[s, r, :] = h
            acum_ref[s, r, :] = acc_a

        def positions(first_step):
            def run():
                def project_gates(p):
                    i = p % 2
                    for r in slabs:
                        conv(p, r, first_step)
                    prea_ref[i, :half] = _dot(gw_ref[0], xb_ref[i, :half])
                    prex_ref[i, :half] = _dot(gw_ref[2], xb_ref[i, :half])
                    prea_ref[i, half:] = _dot(gw_ref[1], xb_ref[i, half:])
                    prex_ref[i, half:] = _dot(gw_ref[3], xb_ref[i, half:])

                project_gates(0)
                for p in range(PP):
                    if p + 1 < PP:
                        project_gates(p + 1)
                    for r in slabs:
                        scan(p, r)
            return run

        lax.cond(m == 0, positions(True), positions(False))
        prev_ref[...] = x_ref[...]

        @pl.when(m == NS - 1)
        def _carry_scan():
            lane = lax.broadcasted_iota(jnp.int32, (N, C), 1)
            hc = hcur_ref[...]
            ac = acur_ref[...]
            for k in range(n_levels):
                sh = 1 << k
                hs = _shift_lanes(hc, lane, sh, 0.0)
                as_ = _shift_lanes(ac, lane, sh, 1.0)
                hc = ac * hs + hc
                ac = ac * as_
            hin_ref[...] = _shift_lanes(hc, lane, 1, 0.0)

    @pl.when(j > NS)
    def _finish():
        k = j - NS - 1

        def project(p):
            out_t = _dot(wout_ref[...], ycat_ref[(k + 1) % 2, p])
            out_ref[p] = out_t.T

        def mix(p):
            s = k * PP + p
            h = hloc_ref[s] + acum_ref[s] * hin_ref[...]
            y_lru = h * _gelu(g_ref[p])
            y5 = y5_ref[p]
            gate = _sigmoid(_dot(wglu_ref[...], y5.astype(BF16)) + cst(8))
            ycat_ref[k % 2, p] = jnp.concatenate([y5 * gate, y_lru], axis=0).astype(BF16)

        @pl.when(k == 0)
        def _():
            for p in range(PP):
                mix(p)

        @pl.when((k > 0) & (k < NS))
        def _():
            for p in range(PP):
                project(p)
                mix(p)

        @pl.when(k == NS)
        def _():
            for p in range(PP):
                project(p)


def _mem_kv_kernel(mem_ref, g_ref, wk_ref, wv_ref, k_ref, v_ref):
    mn = _rms_rows(mem_ref[...], g_ref[...]).astype(BF16)
    k_ref[...] = _dot(mn, wk_ref[...]).astype(BF16)
    v_ref[...] = _dot(mn, wv_ref[...]).astype(BF16)


def _attn_kernel(x_ref, mix_ref, g_ref, wq_ref, wo_ref, k_ref, v_ref, out_ref, *, heads):
    mix = pltpu.einshape("scd->csd", mix_ref[...])
    x = x_ref[...] + mix.reshape(x_ref.shape)
    hn = _rms_rows(x, g_ref[...]).astype(BF16)
    q = _dot(hn, wq_ref[...])
    dh = q.shape[1] // heads
    scale = dh ** -0.5
    outs = []
    for h in range(heads):
        sl = slice(h * dh, (h + 1) * dh)
        sc = _nt_dot(q[:, sl].astype(BF16), k_ref[:, sl]) * scale
        p = jnp.exp(sc - jnp.max(sc, axis=-1, keepdims=True))
        denom = jnp.sum(p, axis=-1, keepdims=True)
        outs.append(_dot(p.astype(BF16), v_ref[:, sl]) / denom)
    o = jnp.concatenate(outs, axis=1).astype(BF16)
    out_ref[...] = x + _dot(o, wo_ref[...])


def _ffn_kernel(x_ref, gf_ref, gfin_ref, wup_ref, cw_ref, wdn_ref, out_ref, halo_ref, prod_ref):
    i = pl.program_id(1)
    M = x_ref.shape[0]
    F = prod_ref.shape[1]
    x = x_ref[...]
    hn = _rms_rows(x, gf_ref[...]).astype(BF16)

    @pl.when(i == 0)
    def _sequence_start():
        halo_ref[...] = jnp.zeros_like(halo_ref)

    row = lax.broadcasted_iota(jnp.int32, (8, FFN_COLS), 0)

    def conv(c0):
        cols = slice(c0, c0 + FFN_COLS)
        z = _dot(hn, wup_ref[:, cols])
        prev = halo_ref[:, cols]
        halo_ref[:, cols] = z[M - 8:]
        z1 = pltpu.roll(z, 1, 0)
        z2 = pltpu.roll(z, 2, 0)
        z1 = jnp.concatenate([jnp.where(row < 1, pltpu.roll(prev, 1, 0), z1[:8]), z1[8:]], axis=0)
        z2 = jnp.concatenate([jnp.where(row < 2, pltpu.roll(prev, 2, 0), z2[:8]), z2[8:]], axis=0)
        cw = cw_ref[:, cols]
        return cw[2:3] * z + cw[1:2] * z1 + cw[0:1] * z2 + cw[3:4]

    for f in range(F // FFN_COLS):
        val = conv(f * FFN_COLS)
        gate = conv(F + f * FFN_COLS)
        prod_ref[:, f * FFN_COLS:(f + 1) * FFN_COLS] = (_gelu(gate) * val).astype(BF16)
    y = x + _dot(prod_ref[...], wdn_ref[...])
    out_ref[...] = _rms_rows(y, gfin_ref[...])


def _s5_matrices(lam_re, lam_im, log_dt, b_re, b_im, c_re, c_im, T, n_levels, C):
    hp = lax.Precision.HIGHEST
    G, P = lam_re.shape
    H = b_re.shape[-1]
    lam = lax.complex(lam_re.astype(F32), lam_im.astype(F32))
    ldt = lam * jnp.exp(log_dt.astype(F32))[:, None]
    lam_bar = jnp.exp(ldt)
    b_bar = ((lam_bar - 1.0) / lam)[..., None] * lax.complex(b_re.astype(F32), b_im.astype(F32))
    lam_pow = jnp.exp(ldt[..., None] * jnp.arange(T + 1, dtype=F32))
    rep = jnp.repeat(jnp.eye(T, dtype=F32), H, axis=1)
    til = jnp.tile(jnp.eye(H, dtype=F32), (1, T))

    def expand(a, m, spec):
        return (jnp.einsum(spec, jnp.real(a), m, precision=hp), jnp.einsum(spec, jnp.imag(a), m, precision=hp))

    def cmul(a, b):
        return a[0] * b[0] - a[1] * b[1], a[0] * b[1] + a[1] * b[0]

    bb = expand(b_bar, til, "gpk,kc->gpc")
    lam_rev = expand(lam_pow[..., :T][..., ::-1], rep, "gpt,tc->gpc")
    lam_fwd = expand(lam_pow[..., :T], rep, "gpt,tc->gpc")
    bpow = jnp.concatenate(cmul(lam_rev, bb), axis=1)
    mk = cmul(lam_fwd, bb)
    kall = (jnp.einsum("ghp,gpc->ghc", c_re.astype(F32), mk[0], precision=hp)
            - jnp.einsum("ghp,gpc->ghc", c_im.astype(F32), mk[1], precision=hp))
    col = jnp.arange(T * H)
    shift = ((col[None, :, None] % H == col[None, None, :] % H)
             & (col[None, :, None] // H == jnp.arange(T)[:, None, None] - col[None, None, :] // H)).astype(F32)
    toep = jnp.einsum("ghj,tjc->gthc", kall, shift, precision=hp).reshape(G, T * H, T * H)
    lam_t = lam_pow[..., 1:].transpose(0, 2, 1)
    lam_rows = expand(lam_t, rep.T, "gtp,ct->gcp")
    c_rows = (jnp.einsum("ch,ghp->gcp", til.T, c_re.astype(F32), precision=hp),
              jnp.einsum("ch,ghp->gcp", til.T, c_im.astype(F32), precision=hp))
    cp = cmul(c_rows, lam_rows)
    cpow = jnp.concatenate([cp[0], -cp[1]], axis=2)
    shifts = 2 ** jnp.arange(n_levels)
    ls = jnp.exp(ldt[..., None] * (T * shifts).astype(F32))
    gb = S5_GROUP_BLOCK
    ls = jnp.stack([jnp.real(ls), jnp.imag(ls)], axis=-1)
    ls = ls.reshape(G // gb, gb * P, 2 * n_levels).transpose(0, 2, 1)
    mask = (jnp.arange(C)[None, :] >= jnp.repeat(shifts, 2)[:, None]).astype(F32)
    lamm = ls[..., None] * mask[None, :, None, :]
    return toep.astype(BF16), bpow.astype(BF16), cpow.astype(BF16), lamm


def _block_diag_t(w):
    nh, di, dj = w.shape
    eye = jnp.eye(nh, dtype=w.dtype)
    return jnp.einsum("hij,hg->hjgi", w, eye).reshape(nh * dj, nh * di)


def _const_spec(shape):
    return pl.BlockSpec(shape, lambda *_: (0,) * len(shape), pipeline_mode=pl.Buffered(1))


def _layer(x, mem_k, mem_v, p):
    B, S, D = x.shape
    T = S5_CHUNK
    C = S // T
    n_levels = int(math.log2(C))
    assert C == 1 << n_levels and S % T == 0
    d_s5 = p["s5_w_glu"].shape[0]
    d_lru = p["lru_conv_b"].shape[0]
    assert d_s5 == d_lru
    n_in = d_s5 + 2 * d_lru
    G = p["s5_lam_re"].shape[0]
    H = d_s5 // G
    assert G % S5_GROUP_BLOCK == 0

    assert T % SUBLANES == 0
    x4 = x.reshape(B, C, T, D)
    w_in_t = p["w_in"].T.astype(BF16)
    zt = pl.pallas_call(
        _mix_in_kernel,
        grid=(B, T // SUBLANES),
        in_specs=[pl.BlockSpec((None, C, SUBLANES, D), lambda b, h: (b, 0, h, 0)),
                  _const_spec((1, D)), _const_spec((n_in, D))],
        out_specs=pl.BlockSpec((None, SUBLANES, n_in, C), lambda b, h: (b, h, 0, 0)),
        out_shape=jax.ShapeDtypeStruct((B, T, n_in, C), F32),
        compiler_params=pltpu.CompilerParams(dimension_semantics=("arbitrary",) * 2,
                                             vmem_limit_bytes=62 * MIB),
        name="mix_in",
    )(x4, p["ln_mix_g"].reshape(1, D), w_in_t)

    toep, bpow, cpow, lamm = _s5_matrices(
        p["s5_lam_re"], p["s5_lam_im"], p["s5_log_dt"], p["s5_b_re"], p["s5_b_im"],
        p["s5_c_re"], p["s5_c_im"], T, n_levels, C)
    P2 = bpow.shape[1]
    assert C % LANES == 0
    d_rows = jnp.broadcast_to(jnp.tile(p["s5_d"].astype(F32), (1, T))[:, :, None], (G, T * H, LANES))
    gb = S5_GROUP_BLOCK
    yt = pl.pallas_call(
        functools.partial(_s5_kernel, n_levels=n_levels),
        grid=(G // gb, B),
        in_specs=[pl.BlockSpec((None, T, gb * H, C), lambda g, b: (b, 0, g, 0)),
                  pl.BlockSpec((gb, T * H, T * H), lambda g, b: (g, 0, 0)),
                  pl.BlockSpec((gb, P2, T * H), lambda g, b: (g, 0, 0)),
                  pl.BlockSpec((gb, T * H, P2), lambda g, b: (g, 0, 0)),
                  pl.BlockSpec((None, 2 * n_levels, gb * P2 // 2, C), lambda g, b: (g, 0, 0, 0)),
                  pl.BlockSpec((gb, T * H, LANES), lambda g, b: (g, 0, 0))],
        out_specs=pl.BlockSpec((None, T, gb * H, C), lambda g, b: (b, 0, g, 0)),
        out_shape=jax.ShapeDtypeStruct((B, T, d_s5, C), F32),
        compiler_params=pltpu.CompilerParams(dimension_semantics=("arbitrary", "arbitrary"),
                                             vmem_limit_bytes=48 * MIB),
        name="s5",
    )(zt, toep, bpow, cpow, lamm, d_rows)

    k7 = (-0.5 * LRU_C) * jax.nn.softplus(-p["lru_lam"].astype(F32)).reshape(d_lru)
    cols = [p["lru_conv_w"][k] for k in range(4)] + [
        p["lru_conv_b"], 0.5 * p["lru_b_a"].reshape(d_lru), 0.5 * p["lru_b_x"].reshape(d_lru), k7, p["s5_b_glu"]]
    cst = jnp.broadcast_to(jnp.stack([c.astype(F32) for c in cols])[:, :, None], (len(cols), d_lru, LANES))
    assert C % LANES == 0
    half = d_lru // 2
    wa = _block_diag_t(p["lru_w_a"])
    wx = _block_diag_t(p["lru_w_x"])
    gw = (0.5 * jnp.stack([wa[:half, :half], wa[half:, half:], wx[:half, :half], wx[half:, half:]])).astype(BF16)
    wglu_t = p["s5_w_glu"].T.astype(BF16)
    wout_t = p["w_out"].T.astype(BF16)

    pp = LRU_POS
    ns = T // pp
    assert T % pp == 0 and pp >= 3

    def scanned(j):
        return jnp.where(j == 0, ns - 1, jnp.clip(j - 1, 0, ns - 1))

    def mixed(j):
        return jnp.clip(j - ns - 1, 0, ns - 1)

    def projected(j):
        return jnp.maximum(j - ns - 2, 0)

    mix = pl.pallas_call(
        functools.partial(_lru_kernel, T=T, n_levels=n_levels),
        grid=(B, 2 * ns + 2),
        in_specs=[pl.BlockSpec((None, pp, d_lru, C), lambda b, j: (b, scanned(j), 1, 0)),
                  pl.BlockSpec((None, pp, d_lru, C), lambda b, j: (b, mixed(j), 2, 0)),
                  pl.BlockSpec((None, pp, d_s5, C), lambda b, j: (b, mixed(j), 0, 0)),
                  _const_spec((len(cols), d_lru, LANES)), _const_spec((4, half, half)),
                  _const_spec((d_s5, d_s5)), _const_spec((D, d_s5 + d_lru))],
        out_specs=pl.BlockSpec((None, pp, C, D), lambda b, j: (b, projected(j), 0, 0)),
        out_shape=jax.ShapeDtypeStruct((B, T, C, D), F32),
        scratch_shapes=[pltpu.VMEM((T, d_lru, C), F32), pltpu.VMEM((T, d_lru, C), F32),
                        pltpu.VMEM((d_lru, C), F32), pltpu.VMEM((d_lru, C), F32),
                        pltpu.VMEM((d_lru, C), F32), pltpu.VMEM((pp, d_lru, C), F32),
                        pltpu.VMEM((2, d_lru, C), F32), pltpu.VMEM((2, d_lru, C), BF16),
                        pltpu.VMEM((2, d_lru, C), F32), pltpu.VMEM((2, d_lru, C), F32),
                        pltpu.VMEM((2, pp, d_s5 + d_lru, C), BF16)],
        compiler_params=pltpu.CompilerParams(dimension_semantics=("arbitrary", "arbitrary"),
                                             vmem_limit_bytes=60 * MIB),
        name="lru_out",
    )(zt, zt, yt, cst, gw, wglu_t, wout_t)

    heads = 4
    M = mem_k.shape[1]
    tm = min(ATTN_ROWS, S)
    assert (tm // T) % SUBLANES == 0
    x2 = pl.pallas_call(
        functools.partial(_attn_kernel, heads=heads),
        grid=(B, S // tm),
        in_specs=[pl.BlockSpec((None, tm, D), lambda b, i: (b, i, 0)),
                  pl.BlockSpec((None, T, tm // T, D), lambda b, i: (b, 0, i, 0)),
                  _const_spec((1, D)), _const_spec((D, D)), _const_spec((D, D)),
                  pl.BlockSpec((None, M, D), lambda b, i: (b, 0, 0)),
                  pl.BlockSpec((None, M, D), lambda b, i: (b, 0, 0))],
        out_specs=pl.BlockSpec((None, tm, D), lambda b, i: (b, i, 0)),
        out_shape=jax.ShapeDtypeStruct((B, S, D), F32),
        compiler_params=pltpu.CompilerParams(dimension_semantics=("arbitrary", "arbitrary"),
                                             vmem_limit_bytes=48 * MIB),
        name="xattn",
    )(x, mix, p["ln_xa_g"].reshape(1, D), p["xa_w_q"].astype(BF16), p["xa_w_o"].astype(BF16), mem_k, mem_v)
    return x2


def _ffn_final(x, p, final_g):
    B, S, D = x.shape
    tm = min(FFN_ROWS, S)
    F2 = p["ffn_w_up"].shape[1]
    F = F2 // 2
    assert F % FFN_COLS == 0 and S % tm == 0
    cw = jnp.concatenate([p["ffn_conv_w"].astype(F32), p["ffn_conv_b"].astype(F32)[None, :]], axis=0)
    return pl.pallas_call(
        _ffn_kernel,
        grid=(B, S // tm),
        in_specs=[pl.BlockSpec((None, tm, D), lambda b, i: (b, i, 0)),
                  _const_spec((1, D)), _const_spec((1, D)), _const_spec((D, F2)),
                  _const_spec((4, F2)), _const_spec((F, D))],
        out_specs=pl.BlockSpec((None, tm, D), lambda b, i: (b, i, 0)),
        out_shape=jax.ShapeDtypeStruct((B, S, D), F32),
        scratch_shapes=[pltpu.VMEM((8, F2), F32), pltpu.VMEM((tm, F), BF16)],
        compiler_params=pltpu.CompilerParams(dimension_semantics=("arbitrary", "arbitrary"),
                                             vmem_limit_bytes=56 * MIB),
        name="conv_ffn",
    )(x, p["ln_ffn_g"].reshape(1, D), final_g.reshape(1, D), p["ffn_w_up"].astype(BF16), cw,
      p["ffn_w_down"].astype(BF16))


def kernel(x, mem, mem_norm_g, ln_mix_g, w_in, w_out, s5_lam_re, s5_lam_im, s5_log_dt, s5_b_re, s5_b_im, s5_c_re, s5_c_im, s5_d, s5_w_glu, s5_b_glu, lru_conv_w, lru_conv_b, lru_w_a, lru_b_a, lru_w_x, lru_b_x, lru_lam, ln_xa_g, xa_w_q, xa_w_k, xa_w_v, xa_w_o, ln_ffn_g, ffn_w_up, ffn_conv_w, ffn_conv_b, ffn_w_down, final_norm_g):
    params = dict(ln_mix_g=ln_mix_g, w_in=w_in, w_out=w_out, s5_lam_re=s5_lam_re, s5_lam_im=s5_lam_im,
                  s5_log_dt=s5_log_dt, s5_b_re=s5_b_re, s5_b_im=s5_b_im, s5_c_re=s5_c_re, s5_c_im=s5_c_im,
                  s5_d=s5_d, s5_w_glu=s5_w_glu, s5_b_glu=s5_b_glu, lru_conv_w=lru_conv_w,
                  lru_conv_b=lru_conv_b, lru_w_a=lru_w_a, lru_b_a=lru_b_a, lru_w_x=lru_w_x, lru_b_x=lru_b_x,
                  lru_lam=lru_lam, ln_xa_g=ln_xa_g, xa_w_q=xa_w_q, xa_w_k=xa_w_k, xa_w_v=xa_w_v,
                  xa_w_o=xa_w_o, ln_ffn_g=ln_ffn_g, ffn_w_up=ffn_w_up, ffn_conv_w=ffn_conv_w,
                  ffn_conv_b=ffn_conv_b, ffn_w_down=ffn_w_down)
    depth = ln_mix_g.shape[0]
    assert depth == 1, "the ConvFFN kernel fuses the final norm, which needs a single layer"
    B, M, D = mem.shape
    p = {k: v[0] for k, v in params.items()}
    rows = B * M
    tr = min(MEM_ROWS, rows)
    assert rows % tr == 0
    mem_k, mem_v = pl.pallas_call(
        _mem_kv_kernel,
        grid=(rows // tr,),
        in_specs=[pl.BlockSpec((tr, D), lambda i: (i, 0)),
                  _const_spec((1, D)), _const_spec((D, D)), _const_spec((D, D))],
        out_specs=[pl.BlockSpec((tr, D), lambda i: (i, 0)), pl.BlockSpec((tr, D), lambda i: (i, 0))],
        out_shape=[jax.ShapeDtypeStruct((rows, D), BF16), jax.ShapeDtypeStruct((rows, D), BF16)],
        compiler_params=pltpu.CompilerParams(dimension_semantics=("arbitrary",),
                                             vmem_limit_bytes=32 * MIB),
        name="mem_kv",
    )(mem.reshape(rows, D), mem_norm_g.reshape(1, D), p["xa_w_k"].astype(BF16), p["xa_w_v"].astype(BF16))
    x2 = _layer(x, mem_k.reshape(B, M, D), mem_v.reshape(B, M, D), p)
    return _ffn_final(x2, p, final_norm_g)
```

```python
import functools
import math

import jax
import jax.numpy as jnp
from jax import lax
from jax.experimental import pallas as pl
from jax.experimental.pallas import tpu as pltpu

EPS = 1e-6
LRU_C = 8.0
S5_CHUNK = 16
FFN_ROWS = 1024
FFN_COLS = 256
ATTN_ROWS = 1024
MEM_ROWS = 1024
S5_GROUP_BLOCK = 8
LRU_ROW_BLOCK = 32
LRU_POS = 4
XA_HEADS = 4
SUBLANES = 8
LANES = 128
MIB = 1024 * 1024
VMEM_LIMIT = {
    "mem_kv": 32 * MIB,
    "mix_in": 62 * MIB,
    "s5": 48 * MIB,
    "lru_out": 60 * MIB,
    "xattn": 48 * MIB,
    "conv_ffn": 56 * MIB,
}

F32 = jnp.float32
BF16 = jnp.bfloat16


def _gelu(x):
    half = 0.5 * x
    return half + half * jnp.tanh(x * (0.7978845608028654 + (0.7978845608028654 * 0.044715) * (x * x)))


def _sigmoid(x):
    return 0.5 * jnp.tanh(0.5 * x) + 0.5


def _roll_lanes(v, shift):
    C = v.shape[1]
    if shift % LANES == 0:
        return jnp.concatenate([v[:, C - shift:], v[:, :C - shift]], axis=1)
    return pltpu.roll(v, shift, 1)


def _rms_rows(x, g):
    ms = jnp.mean(x * x, axis=-1, keepdims=True)
    return x * lax.rsqrt(ms + EPS) * g


def _shift_lanes(v, lane, shift, fill):
    return jnp.where(lane >= shift, _roll_lanes(v, shift), fill)


def _nt_dot(a, b):
    return lax.dot_general(a, b, (((1,), (1,)), ((), ())), preferred_element_type=F32)


def _dot(a, b):
    return jnp.dot(a, b, preferred_element_type=F32)


def _mix_in_kernel(x_ref, g_ref, w_ref, zt_ref):
    xt = pltpu.einshape("cid->icd", x_ref[...])
    for i in range(SUBLANES):
        hn = _rms_rows(xt[i], g_ref[...]).astype(BF16)
        zt_ref[i] = _nt_dot(w_ref[...], hn)


def _s5_kernel(zt_ref, toep_ref, bpow_ref, cpow_ref, lam_ref, d_ref, yt_ref, *, n_levels):
    T, rows, C = zt_ref.shape
    GB = S5_GROUP_BLOCK
    H = rows // GB
    P = bpow_ref.shape[1] // 2

    def group_input(gi):
        return zt_ref[:, gi * H:(gi + 1) * H, :].reshape(T * H, C)

    ends = [_dot(bpow_ref[gi], group_input(gi).astype(BF16)) for gi in range(GB)]
    hr = jnp.concatenate([e[:P] for e in ends], axis=0)
    hi = jnp.concatenate([e[P:] for e in ends], axis=0)
    for k in range(n_levels):
        mr, mi = lam_ref[2 * k], lam_ref[2 * k + 1]
        rr = _roll_lanes(hr, 1 << k)
        ri = _roll_lanes(hi, 1 << k)
        hr, hi = hr + mr * rr - mi * ri, hi + mr * ri + mi * rr
    lane = lax.broadcasted_iota(jnp.int32, (GB * P, C), 1)
    hr = _shift_lanes(hr, lane, 1, 0.0)
    hi = _shift_lanes(hi, lane, 1, 0.0)
    for gi in range(GB):
        u32 = group_input(gi)
        hst = jnp.concatenate([hr[gi * P:(gi + 1) * P], hi[gi * P:(gi + 1) * P]], axis=0).astype(BF16)
        skip = jnp.tile(d_ref[gi], (1, C // LANES)) * u32
        y = _dot(toep_ref[gi], u32.astype(BF16)) + _dot(cpow_ref[gi], hst) + skip
        yt_ref[:, gi * H:(gi + 1) * H, :] = _gelu(y).reshape(T, H, C)


def _lru_kernel(x_ref, g_ref, y5_ref, cst_ref, gw_ref, wglu_ref, wout_ref, out_ref,
                hloc_ref, acum_ref, hcur_ref, acur_ref, hin_ref, prev_ref,
                xc_ref, xb_ref, prea_ref, prex_ref, ycat_ref, *, T, n_levels):
    j = pl.program_id(1)
    PP = x_ref.shape[0]
    NS = T // PP
    N, C = hcur_ref.shape
    half = N // 2
    RB = LRU_ROW_BLOCK
    lane_rb = lax.broadcasted_iota(jnp.int32, (RB, C), 1)
    slabs = [slice(r0, r0 + RB) for r0 in range(0, N, RB)]

    def cst(k, r=slice(None)):
        return jnp.tile(cst_ref[k, r, :], (1, C // LANES))

    @pl.when(j == 0)
    def _halo():
        prev_ref[...] = x_ref[...]

    @pl.when((j >= 1) & (j <= NS))
    def _local_scan():
        m = j - 1

        @pl.when(m == 0)
        def _():
            hcur_ref[...] = jnp.zeros_like(hcur_ref)
            acur_ref[...] = jnp.ones_like(acur_ref)

        def conv(p, r, first_step):
            acc = cst(3, r) * x_ref[p, r, :] + cst(4, r)
            for k in range(1, 4):
                if p - k >= 0:
                    v = x_ref[p - k, r, :]
                else:
                    v = prev_ref[p - k + PP, r, :]
                    if first_step:
                        v = _shift_lanes(v, lane_rb, 1, 0.0)
                acc = acc + cst(3 - k, r) * v
            xc_ref[p % 2, r, :] = acc
            xb_ref[p % 2, r, :] = acc.astype(BF16)

        def scan(p, r):
            s = m * PP + p
            ta = jnp.tanh(prea_ref[p % 2, r, :] + cst(5, r))
            tx = jnp.tanh(prex_ref[p % 2, r, :] + cst(6, r))
            k7 = cst(7, r)
            log_a = k7 * ta + k7
            a = jnp.exp(log_a)
            hx = 0.5 * xc_ref[p % 2, r, :]
            bx = jnp.sqrt(-jnp.tanh(log_a) * (a * a + 1.0)) * (hx * tx + hx)
            h = a * hcur_ref[r, :] + bx
            acc_a = a * acur_ref[r, :]
            hcur_ref[r, :] = h
            acur_ref[r, :] = acc_a
            hloc_ref[s, r, :] = h
            acum_ref[s, r, :] = acc_a

        def positions(first_step):
            def run():
                def project_gates(p):
                    i = p % 2
                    for r in slabs:
                        conv(p, r, first_step)
                    prea_ref[i, :half] = _dot(gw_ref[0], xb_ref[i, :half])
                    prex_ref[i, :half] = _dot(gw_ref[2], xb_ref[i, :half])
                    prea_ref[i, half:] = _dot(gw_ref[1], xb_ref[i, half:])
                    prex_ref[i, half:] = _dot(gw_ref[3], xb_ref[i, half:])

                project_gates(0)
                for p in range(PP):
                    if p + 1 < PP:
                        project_gates(p + 1)
                    for r in slabs:
                        scan(p, r)
            return run

        lax.cond(m == 0, positions(True), positions(False))
        prev_ref[...] = x_ref[...]

        @pl.when(m == NS - 1)
        def _carry_scan():
            lane = lax.broadcasted_iota(jnp.int32, (N, C), 1)
            hc = hcur_ref[...]
            ac = acur_ref[...]
            for k in range(n_levels):
                sh = 1 << k
                hs = _shift_lanes(hc, lane, sh, 0.0)
                as_ = _shift_lanes(ac, lane, sh, 1.0)
                hc = ac * hs + hc
                ac = ac * as_
            hin_ref[...] = _shift_lanes(hc, lane, 1, 0.0)

    @pl.when(j > NS)
    def _finish():
        k = j - NS - 1

        def project(p):
            out_t = _dot(wout_ref[...], ycat_ref[(k + 1) % 2, p])
            out_ref[p] = out_t.T

        def mix(p):
            s = k * PP + p
            h = hloc_ref[s] + acum_ref[s] * hin_ref[...]
            y_lru = h * _gelu(g_ref[p])
            y5 = y5_ref[p]
            gate = _sigmoid(_dot(wglu_ref[...], y5.astype(BF16)) + cst(8))
            ycat_ref[k % 2, p] = jnp.concatenate([y5 * gate, y_lru], axis=0).astype(BF16)

        @pl.when(k == 0)
        def _():
            for p in range(PP):
                mix(p)

        @pl.when((k > 0) & (k < NS))
        def _():
            for p in range(PP):
                project(p)
                mix(p)

        @pl.when(k == NS)
        def _():
            for p in range(PP):
                project(p)


def _mem_kv_kernel(mem_ref, g_ref, wk_ref, wv_ref, k_ref, v_ref):
    mn = _rms_rows(mem_ref[...], g_ref[...]).astype(BF16)
    k_ref[...] = _dot(mn, wk_ref[...]).astype(BF16)
    v_ref[...] = _dot(mn, wv_ref[...]).astype(BF16)


def _attn_kernel(x_ref, mix_ref, g_ref, wq_ref, wo_ref, k_ref, v_ref, out_ref, *, heads):
    mix = pltpu.einshape("scd->csd", mix_ref[...])
    x = x_ref[...] + mix.reshape(x_ref.shape)
    hn = _rms_rows(x, g_ref[...]).astype(BF16)
    q = _dot(hn, wq_ref[...])
    dh = q.shape[1] // heads
    scale = dh ** -0.5
    outs = []
    for h in range(heads):
        sl = slice(h * dh, (h + 1) * dh)
        sc = _nt_dot(q[:, sl].astype(BF16), k_ref[:, sl]) * scale
        p = jnp.exp(sc - jnp.max(sc, axis=-1, keepdims=True))
        denom = jnp.sum(p, axis=-1, keepdims=True)
        outs.append(_dot(p.astype(BF16), v_ref[:, sl]) / denom)
    o = jnp.concatenate(outs, axis=1).astype(BF16)
    out_ref[...] = x + _dot(o, wo_ref[...])


def _ffn_kernel(x_ref, gf_ref, gfin_ref, wup_ref, cw_ref, wdn_ref, out_ref, halo_ref, prod_ref):
    i = pl.program_id(1)
    M = x_ref.shape[0]
    F = prod_ref.shape[1]
    x = x_ref[...]
    hn = _rms_rows(x, gf_ref[...]).astype(BF16)

    @pl.when(i == 0)
    def _sequence_start():
        halo_ref[...] = jnp.zeros_like(halo_ref)

    row = lax.broadcasted_iota(jnp.int32, (8, FFN_COLS), 0)

    def conv(c0):
        cols = slice(c0, c0 + FFN_COLS)
        z = _dot(hn, wup_ref[:, cols])
        prev = halo_ref[:, cols]
        halo_ref[:, cols] = z[M - 8:]
        z1 = pltpu.roll(z, 1, 0)
        z2 = pltpu.roll(z, 2, 0)
        z1 = jnp.concatenate([jnp.where(row < 1, pltpu.roll(prev, 1, 0), z1[:8]), z1[8:]], axis=0)
        z2 = jnp.concatenate([jnp.where(row < 2, pltpu.roll(prev, 2, 0), z2[:8]), z2[8:]], axis=0)
        cw = cw_ref[:, cols]
        return cw[2:3] * z + cw[1:2] * z1 + cw[0:1] * z2 + cw[3:4]

    for f in range(F // FFN_COLS):
        val = conv(f * FFN_COLS)
        gate = conv(F + f * FFN_COLS)
        prod_ref[:, f * FFN_COLS:(f + 1) * FFN_COLS] = (_gelu(gate) * val).astype(BF16)
    y = x + _dot(prod_ref[...], wdn_ref[...])
    out_ref[...] = _rms_rows(y, gfin_ref[...])


def _s5_matrices(lam_re, lam_im, log_dt, b_re, b_im, c_re, c_im, T, n_levels, C):
    hp = lax.Precision.HIGHEST
    G, P = lam_re.shape
    H = b_re.shape[-1]
    lam = lax.complex(lam_re.astype(F32), lam_im.astype(F32))
    ldt = lam * jnp.exp(log_dt.astype(F32))[:, None]
    lam_bar = jnp.exp(ldt)
    b_bar = ((lam_bar - 1.0) / lam)[..., None] * lax.complex(b_re.astype(F32), b_im.astype(F32))
    lam_pow = jnp.exp(ldt[..., None] * jnp.arange(T + 1, dtype=F32))
    rep = jnp.repeat(jnp.eye(T, dtype=F32), H, axis=1)
    til = jnp.tile(jnp.eye(H, dtype=F32), (1, T))

    def expand(a, m, spec):
        return (jnp.einsum(spec, jnp.real(a), m, precision=hp), jnp.einsum(spec, jnp.imag(a), m, precision=hp))

    def cmul(a, b):
        return a[0] * b[0] - a[1] * b[1], a[0] * b[1] + a[1] * b[0]

    bb = expand(b_bar, til, "gpk,kc->gpc")
    lam_rev = expand(lam_pow[..., :T][..., ::-1], rep, "gpt,tc->gpc")
    lam_fwd = expand(lam_pow[..., :T], rep, "gpt,tc->gpc")
    bpow = jnp.concatenate(cmul(lam_rev, bb), axis=1)
    mk = cmul(lam_fwd, bb)
    kall = (jnp.einsum("ghp,gpc->ghc", c_re.astype(F32), mk[0], precision=hp)
            - jnp.einsum("ghp,gpc->ghc", c_im.astype(F32), mk[1], precision=hp))
    col = jnp.arange(T * H)
    shift = ((col[None, :, None] % H == col[None, None, :] % H)
             & (col[None, :, None] // H == jnp.arange(T)[:, None, None] - col[None, None, :] // H)).astype(F32)
    toep = jnp.einsum("ghj,tjc->gthc", kall, shift, precision=hp).reshape(G, T * H, T * H)
    lam_t = lam_pow[..., 1:].transpose(0, 2, 1)
    lam_rows = expand(lam_t, rep.T, "gtp,ct->gcp")
    c_rows = (jnp.einsum("ch,ghp->gcp", til.T, c_re.astype(F32), precision=hp),
              jnp.einsum("ch,ghp->gcp", til.T, c_im.astype(F32), precision=hp))
    cp = cmul(c_rows, lam_rows)
    cpow = jnp.concatenate([cp[0], -cp[1]], axis=2)
    shifts = 2 ** jnp.arange(n_levels)
    ls = jnp.exp(ldt[..., None] * (T * shifts).astype(F32))
    gb = S5_GROUP_BLOCK
    ls = jnp.stack([jnp.real(ls), jnp.imag(ls)], axis=-1)
    ls = ls.reshape(G // gb, gb * P, 2 * n_levels).transpose(0, 2, 1)
    mask = (jnp.arange(C)[None, :] >= jnp.repeat(shifts, 2)[:, None]).astype(F32)
    lamm = ls[..., None] * mask[None, :, None, :]
    return toep.astype(BF16), bpow.astype(BF16), cpow.astype(BF16), lamm


def _block_diag_t(w):
    nh, di, dj = w.shape
    eye = jnp.eye(nh, dtype=w.dtype)
    return jnp.einsum("hij,hg->hjgi", w, eye).reshape(nh * dj, nh * di)


def _const_spec(shape):
    return pl.BlockSpec(shape, lambda *_: (0,) * len(shape), pipeline_mode=pl.Buffered(1))


def _layer(x, mem_k, mem_v, p):
    B, S, D = x.shape
    T = S5_CHUNK
    C = S // T
    n_levels = int(math.log2(C))
    assert C == 1 << n_levels and S % T == 0
    d_s5 = p["s5_w_glu"].shape[0]
    d_lru = p["lru_conv_b"].shape[0]
    assert d_s5 == d_lru
    n_in = d_s5 + 2 * d_lru
    G = p["s5_lam_re"].shape[0]
    H = d_s5 // G
    assert G % S5_GROUP_BLOCK == 0

    assert T % SUBLANES == 0
    x4 = x.reshape(B, C, T, D)
    w_in_t = p["w_in"].T.astype(BF16)
    zt = pl.pallas_call(
        _mix_in_kernel,
        grid=(B, T // SUBLANES),
        in_specs=[pl.BlockSpec((None, C, SUBLANES, D), lambda b, h: (b, 0, h, 0)),
                  _const_spec((1, D)), _const_spec((n_in, D))],
        out_specs=pl.BlockSpec((None, SUBLANES, n_in, C), lambda b, h: (b, h, 0, 0)),
        out_shape=jax.ShapeDtypeStruct((B, T, n_in, C), F32),
        compiler_params=pltpu.CompilerParams(dimension_semantics=("arbitrary",) * 2,
                                             vmem_limit_bytes=VMEM_LIMIT["mix_in"]),
        name="mix_in",
    )(x4, p["ln_mix_g"].reshape(1, D), w_in_t)

    toep, bpow, cpow, lamm = _s5_matrices(
        p["s5_lam_re"], p["s5_lam_im"], p["s5_log_dt"], p["s5_b_re"], p["s5_b_im"],
        p["s5_c_re"], p["s5_c_im"], T, n_levels, C)
    P2 = bpow.shape[1]
    assert C % LANES == 0
    d_rows = jnp.broadcast_to(jnp.tile(p["s5_d"].astype(F32), (1, T))[:, :, None], (G, T * H, LANES))
    gb = S5_GROUP_BLOCK
    yt = pl.pallas_call(
        functools.partial(_s5_kernel, n_levels=n_levels),
        grid=(G // gb, B),
        in_specs=[pl.BlockSpec((None, T, gb * H, C), lambda g, b: (b, 0, g, 0)),
                  pl.BlockSpec((gb, T * H, T * H), lambda g, b: (g, 0, 0)),
                  pl.BlockSpec((gb, P2, T * H), lambda g, b: (g, 0, 0)),
                  pl.BlockSpec((gb, T * H, P2), lambda g, b: (g, 0, 0)),
                  pl.BlockSpec((None, 2 * n_levels, gb * P2 // 2, C), lambda g, b: (g, 0, 0, 0)),
                  pl.BlockSpec((gb, T * H, LANES), lambda g, b: (g, 0, 0))],
        out_specs=pl.BlockSpec((None, T, gb * H, C), lambda g, b: (b, 0, g, 0)),
        out_shape=jax.ShapeDtypeStruct((B, T, d_s5, C), F32),
        compiler_params=pltpu.CompilerParams(dimension_semantics=("arbitrary", "arbitrary"),
                                             vmem_limit_bytes=VMEM_LIMIT["s5"]),
        name="s5",
    )(zt, toep, bpow, cpow, lamm, d_rows)

    k7 = (-0.5 * LRU_C) * jax.nn.softplus(-p["lru_lam"].astype(F32)).reshape(d_lru)
    cols = [p["lru_conv_w"][k] for k in range(4)] + [
        p["lru_conv_b"], 0.5 * p["lru_b_a"].reshape(d_lru), 0.5 * p["lru_b_x"].reshape(d_lru), k7, p["s5_b_glu"]]
    cst = jnp.broadcast_to(jnp.stack([c.astype(F32) for c in cols])[:, :, None], (len(cols), d_lru, LANES))
    assert C % LANES == 0
    half = d_lru // 2
    wa = _block_diag_t(p["lru_w_a"])
    wx = _block_diag_t(p["lru_w_x"])
    gw = (0.5 * jnp.stack([wa[:half, :half], wa[half:, half:], wx[:half, :half], wx[half:, half:]])).astype(BF16)
    wglu_t = p["s5_w_glu"].T.astype(BF16)
    wout_t = p["w_out"].T.astype(BF16)

    pp = LRU_POS
    ns = T // pp
    assert T % pp == 0 and pp >= 3

    def scanned(j):
        return jnp.where(j == 0, ns - 1, jnp.clip(j - 1, 0, ns - 1))

    def mixed(j):
        return jnp.clip(j - ns - 1, 0, ns - 1)

    def projected(j):
        return jnp.maximum(j - ns - 2, 0)

    mix = pl.pallas_call(
        functools.partial(_lru_kernel, T=T, n_levels=n_levels),
        grid=(B, 2 * ns + 2),
        in_specs=[pl.BlockSpec((None, pp, d_lru, C), lambda b, j: (b, scanned(j), 1, 0)),
                  pl.BlockSpec((None, pp, d_lru, C), lambda b, j: (b, mixed(j), 2, 0)),
                  pl.BlockSpec((None, pp, d_s5, C), lambda b, j: (b, mixed(j), 0, 0)),
                  _const_spec((len(cols), d_lru, LANES)), _const_spec((4, half, half)),
                  _const_spec((d_s5, d_s5)), _const_spec((D, d_s5 + d_lru))],
        out_specs=pl.BlockSpec((None, pp, C, D), lambda b, j: (b, projected(j), 0, 0)),
        out_shape=jax.ShapeDtypeStruct((B, T, C, D), F32),
        scratch_shapes=[pltpu.VMEM((T, d_lru, C), F32), pltpu.VMEM((T, d_lru, C), F32),
                        pltpu.VMEM((d_lru, C), F32), pltpu.VMEM((d_lru, C), F32),
                        pltpu.VMEM((d_lru, C), F32), pltpu.VMEM((pp, d_lru, C), F32),
                        pltpu.VMEM((2, d_lru, C), F32), pltpu.VMEM((2, d_lru, C), BF16),
                        pltpu.VMEM((2, d_lru, C), F32), pltpu.VMEM((2, d_lru, C), F32),
                        pltpu.VMEM((2, pp, d_s5 + d_lru, C), BF16)],
        compiler_params=pltpu.CompilerParams(dimension_semantics=("arbitrary", "arbitrary"),
                                             vmem_limit_bytes=VMEM_LIMIT["lru_out"]),
        name="lru_out",
    )(zt, zt, yt, cst, gw, wglu_t, wout_t)

    heads = XA_HEADS
    M = mem_k.shape[1]
    tm = min(ATTN_ROWS, S)
    assert (tm // T) % SUBLANES == 0
    x2 = pl.pallas_call(
        functools.partial(_attn_kernel, heads=heads),
        grid=(B, S // tm),
        in_specs=[pl.BlockSpec((None, tm, D), lambda b, i: (b, i, 0)),
                  pl.BlockSpec((None, T, tm // T, D), lambda b, i: (b, 0, i, 0)),
                  _const_spec((1, D)), _const_spec((D, D)), _const_spec((D, D)),
                  pl.BlockSpec((None, M, D), lambda b, i: (b, 0, 0)),
                  pl.BlockSpec((None, M, D), lambda b, i: (b, 0, 0))],
        out_specs=pl.BlockSpec((None, tm, D), lambda b, i: (b, i, 0)),
        out_shape=jax.ShapeDtypeStruct((B, S, D), F32),
        compiler_params=pltpu.CompilerParams(dimension_semantics=("arbitrary", "arbitrary"),
                                             vmem_limit_bytes=VMEM_LIMIT["xattn"]),
        name="xattn",
    )(x, mix, p["ln_xa_g"].reshape(1, D), p["xa_w_q"].astype(BF16), p["xa_w_o"].astype(BF16), mem_k, mem_v)
    return x2


def _ffn_final(x, p, final_g):
    B, S, D = x.shape
    tm = min(FFN_ROWS, S)
    F2 = p["ffn_w_up"].shape[1]
    F = F2 // 2
    assert F % FFN_COLS == 0 and S % tm == 0
    cw = jnp.concatenate([p["ffn_conv_w"].astype(F32), p["ffn_conv_b"].astype(F32)[None, :]], axis=0)
    return pl.pallas_call(
        _ffn_kernel,
        grid=(B, S // tm),
        in_specs=[pl.BlockSpec((None, tm, D), lambda b, i: (b, i, 0)),
                  _const_spec((1, D)), _const_spec((1, D)), _const_spec((D, F2)),
                  _const_spec((4, F2)), _const_spec((F, D))],
        out_specs=pl.BlockSpec((None, tm, D), lambda b, i: (b, i, 0)),
        out_shape=jax.ShapeDtypeStruct((B, S, D), F32),
        scratch_shapes=[pltpu.VMEM((8, F2), F32), pltpu.VMEM((tm, F), BF16)],
        compiler_params=pltpu.CompilerParams(dimension_semantics=("arbitrary", "arbitrary"),
                                             vmem_limit_bytes=VMEM_LIMIT["conv_ffn"]),
        name="conv_ffn",
    )(x, p["ln_ffn_g"].reshape(1, D), final_g.reshape(1, D), p["ffn_w_up"].astype(BF16), cw,
      p["ffn_w_down"].astype(BF16))


def kernel(x, mem, mem_norm_g, ln_mix_g, w_in, w_out, s5_lam_re, s5_lam_im, s5_log_dt, s5_b_re, s5_b_im, s5_c_re, s5_c_im, s5_d, s5_w_glu, s5_b_glu, lru_conv_w, lru_conv_b, lru_w_a, lru_b_a, lru_w_x, lru_b_x, lru_lam, ln_xa_g, xa_w_q, xa_w_k, xa_w_v, xa_w_o, ln_ffn_g, ffn_w_up, ffn_conv_w, ffn_conv_b, ffn_w_down, final_norm_g):
    params = dict(ln_mix_g=ln_mix_g, w_in=w_in, w_out=w_out, s5_lam_re=s5_lam_re, s5_lam_im=s5_lam_im,
                  s5_log_dt=s5_log_dt, s5_b_re=s5_b_re, s5_b_im=s5_b_im, s5_c_re=s5_c_re, s5_c_im=s5_c_im,
                  s5_d=s5_d, s5_w_glu=s5_w_glu, s5_b_glu=s5_b_glu, lru_conv_w=lru_conv_w,
                  lru_conv_b=lru_conv_b, lru_w_a=lru_w_a, lru_b_a=lru_b_a, lru_w_x=lru_w_x, lru_b_x=lru_b_x,
                  lru_lam=lru_lam, ln_xa_g=ln_xa_g, xa_w_q=xa_w_q, xa_w_k=xa_w_k, xa_w_v=xa_w_v,
                  xa_w_o=xa_w_o, ln_ffn_g=ln_ffn_g, ffn_w_up=ffn_w_up, ffn_conv_w=ffn_conv_w,
                  ffn_conv_b=ffn_conv_b, ffn_w_down=ffn_w_down)
    depth = ln_mix_g.shape[0]
    assert depth == 1, "the ConvFFN kernel fuses the final norm, which needs a single layer"
    B, M, D = mem.shape
    p = {k: v[0] for k, v in params.items()}
    rows = B * M
    tr = min(MEM_ROWS, rows)
    assert rows % tr == 0
    mem_k, mem_v = pl.pallas_call(
        _mem_kv_kernel,
        grid=(rows // tr,),
        in_specs=[pl.BlockSpec((tr, D), lambda i: (i, 0)),
                  _const_spec((1, D)), _const_spec((D, D)), _const_spec((D, D))],
        out_specs=[pl.BlockSpec((tr, D), lambda i: (i, 0)), pl.BlockSpec((tr, D), lambda i: (i, 0))],
        out_shape=[jax.ShapeDtypeStruct((rows, D), BF16), jax.ShapeDtypeStruct((rows, D), BF16)],
        compiler_params=pltpu.CompilerParams(dimension_semantics=("arbitrary",),
                                             vmem_limit_bytes=VMEM_LIMIT["mem_kv"]),
        name="mem_kv",
    )(mem.reshape(rows, D), mem_norm_g.reshape(1, D), p["xa_w_k"].astype(BF16), p["xa_w_v"].astype(BF16))
    x2 = _layer(x, mem_k.reshape(B, M, D), mem_v.reshape(B, M, D), p)
    return _ffn_final(x2, p, final_norm_g)
```

```python
import functools
import math

import jax
import jax.numpy as jnp
from jax import lax
from jax.experimental import pallas as pl
from jax.experimental.pallas import tpu as pltpu

EPS = 1e-6
LRU_C = 8.0
S5_CHUNK = 16
FFN_ROWS = 1024
FFN_COLS = 256
ATTN_ROWS = 1024
MEM_ROWS = 1024
S5_GROUP_BLOCK = 8
LRU_ROW_BLOCK = 32
LRU_POS = 4
XA_HEADS = 4
SUBLANES = 8
LANES = 128
MIB = 1024 * 1024
VMEM_LIMIT = {
    "mem_kv": 32 * MIB,
    "mix_in": 62 * MIB,
    "s5": 48 * MIB,
    "lru_out": 60 * MIB,
    "xattn": 48 * MIB,
    "conv_ffn": 56 * MIB,
}

F32 = jnp.float32
BF16 = jnp.bfloat16


def _gelu(x):
    half = 0.5 * x
    return half + half * jnp.tanh(x * (0.7978845608028654 + (0.7978845608028654 * 0.044715) * (x * x)))


def _sigmoid(x):
    return 0.5 * jnp.tanh(0.5 * x) + 0.5


def _roll_lanes(v, shift):
    C = v.shape[1]
    if shift % LANES == 0:
        return jnp.concatenate([v[:, C - shift:], v[:, :C - shift]], axis=1)
    return pltpu.roll(v, shift, 1)


def _rms_rows(x, g):
    ms = jnp.mean(x * x, axis=-1, keepdims=True)
    return x * lax.rsqrt(ms + EPS) * g


def _shift_lanes(v, lane, shift, fill):
    return jnp.where(lane >= shift, _roll_lanes(v, shift), fill)


def _nt_dot(a, b):
    return lax.dot_general(a, b, (((1,), (1,)), ((), ())), preferred_element_type=F32)


def _dot(a, b):
    return jnp.dot(a, b, preferred_element_type=F32)


def _mix_in_kernel(x_ref, g_ref, w_ref, zt_ref):
    xt = pltpu.einshape("cid->icd", x_ref[...])
    for i in range(SUBLANES):
        hn = _rms_rows(xt[i], g_ref[...]).astype(BF16)
        zt_ref[i] = _nt_dot(w_ref[...], hn)


def _s5_kernel(zt_ref, toep_ref, bpow_ref, cpow_ref, lam_ref, d_ref, yt_ref, *, n_levels):
    T, rows, C = zt_ref.shape
    GB = S5_GROUP_BLOCK
    H = rows // GB
    P = bpow_ref.shape[1] // 2

    def group_input(gi):
        return zt_ref[:, gi * H:(gi + 1) * H, :].reshape(T * H, C)

    ends = [_dot(bpow_ref[gi], group_input(gi).astype(BF16)) for gi in range(GB)]
    hr = jnp.concatenate([e[:P] for e in ends], axis=0)
    hi = jnp.concatenate([e[P:] for e in ends], axis=0)
    for k in range(n_levels):
        mr, mi = lam_ref[2 * k], lam_ref[2 * k + 1]
        rr = _roll_lanes(hr, 1 << k)
        ri = _roll_lanes(hi, 1 << k)
        hr, hi = hr + mr * rr - mi * ri, hi + mr * ri + mi * rr
    lane = lax.broadcasted_iota(jnp.int32, (GB * P, C), 1)
    hr = _shift_lanes(hr, lane, 1, 0.0)
    hi = _shift_lanes(hi, lane, 1, 0.0)
    for gi in range(GB):
        u32 = group_input(gi)
        hst = jnp.concatenate([hr[gi * P:(gi + 1) * P], hi[gi * P:(gi + 1) * P]], axis=0).astype(BF16)
        skip = jnp.tile(d_ref[gi], (1, C // LANES)) * u32
        y = _dot(toep_ref[gi], u32.astype(BF16)) + _dot(cpow_ref[gi], hst) + skip
        yt_ref[:, gi * H:(gi + 1) * H, :] = _gelu(y).reshape(T, H, C)


def _lru_kernel(x_ref, g_ref, y5_ref, cst_ref, gw_ref, wglu_ref, wout_ref, out_ref,
                hloc_ref, acum_ref, hcur_ref, acur_ref, hin_ref, prev_ref,
                xc_ref, xb_ref, prea_ref, prex_ref, ycat_ref, *, T, n_levels):
    j = pl.program_id(1)
    PP = x_ref.shape[0]
    NS = T // PP
    N, C = hcur_ref.shape
    half = N // 2
    RB = LRU_ROW_BLOCK
    lane_rb = lax.broadcasted_iota(jnp.int32, (RB, C), 1)
    slabs = [slice(r0, r0 + RB) for r0 in range(0, N, RB)]

    def cst(k, r=slice(None)):
        return jnp.tile(cst_ref[k, r, :], (1, C // LANES))

    @pl.when(j == 0)
    def _halo():
        prev_ref[...] = x_ref[...]

    @pl.when((j >= 1) & (j <= NS))
    def _local_scan():
        m = j - 1

        @pl.when(m == 0)
        def _():
            hcur_ref[...] = jnp.zeros_like(hcur_ref)
            acur_ref[...] = jnp.ones_like(acur_ref)

        def conv(p, r, first_step):
            acc = cst(3, r) * x_ref[p, r, :] + cst(4, r)
            for k in range(1, 4):
                if p - k >= 0:
                    v = x_ref[p - k, r, :]
                else:
                    v = prev_ref[p - k + PP, r, :]
                    if first_step:
                        v = _shift_lanes(v, lane_rb, 1, 0.0)
                acc = acc + cst(3 - k, r) * v
            xc_ref[p % 2, r, :] = acc
            xb_ref[p % 2, r, :] = acc.astype(BF16)

        def scan(p, r):
            s = m * PP + p
            ta = jnp.tanh(prea_ref[p % 2, r, :] + cst(5, r))
            tx = jnp.tanh(prex_ref[p % 2, r, :] + cst(6, r))
            k7 = cst(7, r)
            log_a = k7 * ta + k7
            a = jnp.exp(log_a)
            hx = 0.5 * xc_ref[p % 2, r, :]
            bx = jnp.sqrt(-jnp.tanh(log_a) * (a * a + 1.0)) * (hx * tx + hx)
            h = a * hcur_ref[r, :] + bx
            acc_a = a * acur_ref[r, :]
            hcur_ref[r, :] = h
            acur_ref[r, :] = acc_a
            hloc_ref[s, r, :] = h
            acum_ref[s, r, :] = acc_a

        def positions(first_step):
            def run():
                def project_gates(p):
                    i = p % 2
                    for r in slabs:
                        conv(p, r, first_step)
                    prea_ref[i, :half] = _dot(gw_ref[0], xb_ref[i, :half])
                    prex_ref[i, :half] = _dot(gw_ref[2], xb_ref[i, :half])
                    prea_ref[i, half:] = _dot(gw_ref[1], xb_ref[i, half:])
                    prex_ref[i, half:] = _dot(gw_ref[3], xb_ref[i, half:])

                project_gates(0)
                for p in range(PP):
                    if p + 1 < PP:
                        project_gates(p + 1)
                    for r in slabs:
                        scan(p, r)
            return run

        lax.cond(m == 0, positions(True), positions(False))
        prev_ref[...] = x_ref[...]

        @pl.when(m == NS - 1)
        def _carry_scan():
            lane = lax.broadcasted_iota(jnp.int32, (N, C), 1)
            hc = hcur_ref[...]
            ac = acur_ref[...]
            for k in range(n_levels):
                sh = 1 << k
                hs = _shift_lanes(hc, lane, sh, 0.0)
                as_ = _shift_lanes(ac, lane, sh, 1.0)
                hc = ac * hs + hc
                ac = ac * as_
            hin_ref[...] = _shift_lanes(hc, lane, 1, 0.0)

    @pl.when(j > NS)
    def _finish():
        k = j - NS - 1

        def project(p):
            out_t = _dot(wout_ref[...], ycat_ref[(k + 1) % 2, p])
            out_ref[p] = out_t.T

        def mix(p):
            s = k * PP + p
            h = hloc_ref[s] + acum_ref[s] * hin_ref[...]
            y_lru = h * _gelu(g_ref[p])
            y5 = y5_ref[p]
            gate = _sigmoid(_dot(wglu_ref[...], y5.astype(BF16)) + cst(8))
            ycat_ref[k % 2, p] = jnp.concatenate([y5 * gate, y_lru], axis=0).astype(BF16)

        @pl.when(k == 0)
        def _():
            for p in range(PP):
                mix(p)

        @pl.when((k > 0) & (k < NS))
        def _():
            for p in range(PP):
                project(p)
                mix(p)

        @pl.when(k == NS)
        def _():
            for p in range(PP):
                project(p)


def _mem_kv_kernel(mem_ref, g_ref, wk_ref, wv_ref, k_ref, v_ref):
    mn = _rms_rows(mem_ref[...], g_ref[...]).astype(BF16)
    k_ref[...] = _dot(mn, wk_ref[...]).astype(BF16)
    v_ref[...] = _dot(mn, wv_ref[...]).astype(BF16)


def _attn_kernel(x_ref, mix_ref, g_ref, wq_ref, wo_ref, k_ref, v_ref, out_ref, *, heads):
    mix = pltpu.einshape("scd->csd", mix_ref[...])
    x = x_ref[...] + mix.reshape(x_ref.shape)
    hn = _rms_rows(x, g_ref[...]).astype(BF16)
    q = _dot(hn, wq_ref[...])
    dh = q.shape[1] // heads
    scale = dh ** -0.5
    outs = []
    for h in range(heads):
        sl = slice(h * dh, (h + 1) * dh)
        sc = _nt_dot(q[:, sl].astype(BF16), k_ref[:, sl]) * scale
        p = jnp.exp(sc - jnp.max(sc, axis=-1, keepdims=True))
        denom = jnp.sum(p, axis=-1, keepdims=True)
        outs.append(_dot(p.astype(BF16), v_ref[:, sl]) / denom)
    o = jnp.concatenate(outs, axis=1).astype(BF16)
    out_ref[...] = x + _dot(o, wo_ref[...])


def _ffn_kernel(x_ref, gf_ref, gfin_ref, wup_ref, cw_ref, wdn_ref, out_ref, halo_ref, prod_ref):
    i = pl.program_id(1)
    M = x_ref.shape[0]
    F = prod_ref.shape[1]
    x = x_ref[...]
    hn = _rms_rows(x, gf_ref[...]).astype(BF16)

    @pl.when(i == 0)
    def _sequence_start():
        halo_ref[...] = jnp.zeros_like(halo_ref)

    row = lax.broadcasted_iota(jnp.int32, (8, FFN_COLS), 0)

    def conv(c0):
        cols = slice(c0, c0 + FFN_COLS)
        z = _dot(hn, wup_ref[:, cols])
        prev = halo_ref[:, cols]
        halo_ref[:, cols] = z[M - 8:]
        z1 = pltpu.roll(z, 1, 0)
        z2 = pltpu.roll(z, 2, 0)
        z1 = jnp.concatenate([jnp.where(row < 1, pltpu.roll(prev, 1, 0), z1[:8]), z1[8:]], axis=0)
        z2 = jnp.concatenate([jnp.where(row < 2, pltpu.roll(prev, 2, 0), z2[:8]), z2[8:]], axis=0)
        cw = cw_ref[:, cols]
        return cw[2:3] * z + cw[1:2] * z1 + cw[0:1] * z2 + cw[3:4]

    for f in range(F // FFN_COLS):
        val = conv(f * FFN_COLS)
        gate = conv(F + f * FFN_COLS)
        prod_ref[:, f * FFN_COLS:(f + 1) * FFN_COLS] = (_gelu(gate) * val).astype(BF16)
    y = x + _dot(prod_ref[...], wdn_ref[...])
    out_ref[...] = _rms_rows(y, gfin_ref[...])


def _s5_matrices(lam_re, lam_im, log_dt, b_re, b_im, c_re, c_im, T, n_levels, C):
    hp = lax.Precision.HIGHEST
    G, P = lam_re.shape
    H = b_re.shape[-1]
    lam = lax.complex(lam_re.astype(F32), lam_im.astype(F32))
    ldt = lam * jnp.exp(log_dt.astype(F32))[:, None]
    lam_bar = jnp.exp(ldt)
    b_bar = ((lam_bar - 1.0) / lam)[..., None] * lax.complex(b_re.astype(F32), b_im.astype(F32))
    lam_pow = jnp.exp(ldt[..., None] * jnp.arange(T + 1, dtype=F32))
    rep = jnp.repeat(jnp.eye(T, dtype=F32), H, axis=1)
    til = jnp.tile(jnp.eye(H, dtype=F32), (1, T))

    def expand(a, m, spec):
        return (jnp.einsum(spec, jnp.real(a), m, precision=hp), jnp.einsum(spec, jnp.imag(a), m, precision=hp))

    def cmul(a, b):
        return a[0] * b[0] - a[1] * b[1], a[0] * b[1] + a[1] * b[0]

    bb = expand(b_bar, til, "gpk,kc->gpc")
    lam_rev = expand(lam_pow[..., :T][..., ::-1], rep, "gpt,tc->gpc")
    lam_fwd = expand(lam_pow[..., :T], rep, "gpt,tc->gpc")
    bpow = jnp.concatenate(cmul(lam_rev, bb), axis=1)
    mk = cmul(lam_fwd, bb)
    kall = (jnp.einsum("ghp,gpc->ghc", c_re.astype(F32), mk[0], precision=hp)
            - jnp.einsum("ghp,gpc->ghc", c_im.astype(F32), mk[1], precision=hp))
    col = jnp.arange(T * H)
    shift = ((col[None, :, None] % H == col[None, None, :] % H)
             & (col[None, :, None] // H == jnp.arange(T)[:, None, None] - col[None, None, :] // H)).astype(F32)
    toep = jnp.einsum("ghj,tjc->gthc", kall, shift, precision=hp).reshape(G, T * H, T * H)
    lam_next = expand(lam_pow[..., 1:], rep, "gpt,tc->gpc")
    c_cols = (jnp.einsum("ghp,hc->gpc", c_re.astype(F32), til, precision=hp),
              jnp.einsum("ghp,hc->gpc", c_im.astype(F32), til, precision=hp))
    cp = cmul(c_cols, lam_next)
    cpow = jnp.concatenate([cp[0], -cp[1]], axis=1).astype(BF16).transpose(0, 2, 1)
    shifts = 2 ** jnp.arange(n_levels)
    ls = jnp.exp(ldt[..., None] * (T * shifts).astype(F32))
    gb = S5_GROUP_BLOCK
    ls = jnp.stack([jnp.real(ls), jnp.imag(ls)], axis=-1)
    ls = ls.reshape(G // gb, gb * P, 2 * n_levels).transpose(0, 2, 1)
    mask = (jnp.arange(C)[None, :] >= jnp.repeat(shifts, 2)[:, None]).astype(F32)
    lamm = ls[..., None] * mask[None, :, None, :]
    return toep.astype(BF16), bpow.astype(BF16), cpow.astype(BF16), lamm


def _block_diag_t(w):
    nh, di, dj = w.shape
    eye = jnp.eye(nh, dtype=w.dtype)
    return jnp.einsum("hij,hg->hjgi", w, eye).reshape(nh * dj, nh * di)


def _const_spec(shape):
    return pl.BlockSpec(shape, lambda *_: (0,) * len(shape), pipeline_mode=pl.Buffered(1))


def _layer(x, mem_k, mem_v, p):
    B, S, D = x.shape
    T = S5_CHUNK
    C = S // T
    n_levels = int(math.log2(C))
    assert C == 1 << n_levels and S % T == 0
    d_s5 = p["s5_w_glu"].shape[0]
    d_lru = p["lru_conv_b"].shape[0]
    assert d_s5 == d_lru
    n_in = d_s5 + 2 * d_lru
    G = p["s5_lam_re"].shape[0]
    H = d_s5 // G
    assert G % S5_GROUP_BLOCK == 0

    assert T % SUBLANES == 0
    x4 = x.reshape(B, C, T, D)
    w_in_t = p["w_in"].T.astype(BF16)
    zt = pl.pallas_call(
        _mix_in_kernel,
        grid=(B, T // SUBLANES),
        in_specs=[pl.BlockSpec((None, C, SUBLANES, D), lambda b, h: (b, 0, h, 0)),
                  _const_spec((1, D)), _const_spec((n_in, D))],
        out_specs=pl.BlockSpec((None, SUBLANES, n_in, C), lambda b, h: (b, h, 0, 0)),
        out_shape=jax.ShapeDtypeStruct((B, T, n_in, C), F32),
        compiler_params=pltpu.CompilerParams(dimension_semantics=("arbitrary",) * 2,
                                             vmem_limit_bytes=VMEM_LIMIT["mix_in"]),
        name="mix_in",
    )(x4, p["ln_mix_g"].reshape(1, D), w_in_t)

    toep, bpow, cpow, lamm = _s5_matrices(
        p["s5_lam_re"], p["s5_lam_im"], p["s5_log_dt"], p["s5_b_re"], p["s5_b_im"],
        p["s5_c_re"], p["s5_c_im"], T, n_levels, C)
    P2 = bpow.shape[1]
    assert C % LANES == 0
    d_rows = jnp.broadcast_to(jnp.tile(p["s5_d"].astype(F32), (1, T))[:, :, None], (G, T * H, LANES))
    gb = S5_GROUP_BLOCK
    yt = pl.pallas_call(
        functools.partial(_s5_kernel, n_levels=n_levels),
        grid=(G // gb, B),
        in_specs=[pl.BlockSpec((None, T, gb * H, C), lambda g, b: (b, 0, g, 0)),
                  pl.BlockSpec((gb, T * H, T * H), lambda g, b: (g, 0, 0)),
                  pl.BlockSpec((gb, P2, T * H), lambda g, b: (g, 0, 0)),
                  pl.BlockSpec((gb, T * H, P2), lambda g, b: (g, 0, 0)),
                  pl.BlockSpec((None, 2 * n_levels, gb * P2 // 2, C), lambda g, b: (g, 0, 0, 0)),
                  pl.BlockSpec((gb, T * H, LANES), lambda g, b: (g, 0, 0))],
        out_specs=pl.BlockSpec((None, T, gb * H, C), lambda g, b: (b, 0, g, 0)),
        out_shape=jax.ShapeDtypeStruct((B, T, d_s5, C), F32),
        compiler_params=pltpu.CompilerParams(dimension_semantics=("arbitrary", "arbitrary"),
                                             vmem_limit_bytes=VMEM_LIMIT["s5"]),
        name="s5",
    )(zt, toep, bpow, cpow, lamm, d_rows)

    k7 = (-0.5 * LRU_C) * jax.nn.softplus(-p["lru_lam"].astype(F32)).reshape(d_lru)
    cols = [p["lru_conv_w"][k] for k in range(4)] + [
        p["lru_conv_b"], 0.5 * p["lru_b_a"].reshape(d_lru), 0.5 * p["lru_b_x"].reshape(d_lru), k7, p["s5_b_glu"]]
    cst = jnp.broadcast_to(jnp.stack([c.astype(F32) for c in cols])[:, :, None], (len(cols), d_lru, LANES))
    assert C % LANES == 0
    half = d_lru // 2
    wa = _block_diag_t(p["lru_w_a"])
    wx = _block_diag_t(p["lru_w_x"])
    gw = (0.5 * jnp.stack([wa[:half, :half], wa[half:, half:], wx[:half, :half], wx[half:, half:]])).astype(BF16)
    wglu_t = p["s5_w_glu"].T.astype(BF16)
    wout_t = p["w_out"].T.astype(BF16)

    pp = LRU_POS
    ns = T // pp
    assert T % pp == 0 and pp >= 3

    def scanned(j):
        return jnp.where(j == 0, ns - 1, jnp.clip(j - 1, 0, ns - 1))

    def mixed(j):
        return jnp.clip(j - ns - 1, 0, ns - 1)

    def projected(j):
        return jnp.maximum(j - ns - 2, 0)

    mix = pl.pallas_call(
        functools.partial(_lru_kernel, T=T, n_levels=n_levels),
        grid=(B, 2 * ns + 2),
        in_specs=[pl.BlockSpec((None, pp, d_lru, C), lambda b, j: (b, scanned(j), 1, 0)),
                  pl.BlockSpec((None, pp, d_lru, C), lambda b, j: (b, mixed(j), 2, 0)),
                  pl.BlockSpec((None, pp, d_s5, C), lambda b, j: (b, mixed(j), 0, 0)),
                  _const_spec((len(cols), d_lru, LANES)), _const_spec((4, half, half)),
                  _const_spec((d_s5, d_s5)), _const_spec((D, d_s5 + d_lru))],
        out_specs=pl.BlockSpec((None, pp, C, D), lambda b, j: (b, projected(j), 0, 0)),
        out_shape=jax.ShapeDtypeStruct((B, T, C, D), F32),
        scratch_shapes=[pltpu.VMEM((T, d_lru, C), F32), pltpu.VMEM((T, d_lru, C), F32),
                        pltpu.VMEM((d_lru, C), F32), pltpu.VMEM((d_lru, C), F32),
                        pltpu.VMEM((d_lru, C), F32), pltpu.VMEM((pp, d_lru, C), F32),
                        pltpu.VMEM((2, d_lru, C), F32), pltpu.VMEM((2, d_lru, C), BF16),
                        pltpu.VMEM((2, d_lru, C), F32), pltpu.VMEM((2, d_lru, C), F32),
                        pltpu.VMEM((2, pp, d_s5 + d_lru, C), BF16)],
        compiler_params=pltpu.CompilerParams(dimension_semantics=("arbitrary", "arbitrary"),
                                             vmem_limit_bytes=VMEM_LIMIT["lru_out"]),
        name="lru_out",
    )(zt, zt, yt, cst, gw, wglu_t, wout_t)

    heads = XA_HEADS
    M = mem_k.shape[1]
    tm = min(ATTN_ROWS, S)
    assert (tm // T) % SUBLANES == 0
    x2 = pl.pallas_call(
        functools.partial(_attn_kernel, heads=heads),
        grid=(B, S // tm),
        in_specs=[pl.BlockSpec((None, tm, D), lambda b, i: (b, i, 0)),
                  pl.BlockSpec((None, T, tm // T, D), lambda b, i: (b, 0, i, 0)),
                  _const_spec((1, D)), _const_spec((D, D)), _const_spec((D, D)),
                  pl.BlockSpec((None, M, D), lambda b, i: (b, 0, 0)),
                  pl.BlockSpec((None, M, D), lambda b, i: (b, 0, 0))],
        out_specs=pl.BlockSpec((None, tm, D), lambda b, i: (b, i, 0)),
        out_shape=jax.ShapeDtypeStruct((B, S, D), F32),
        compiler_params=pltpu.CompilerParams(dimension_semantics=("arbitrary", "arbitrary"),
                                             vmem_limit_bytes=VMEM_LIMIT["xattn"]),
        name="xattn",
    )(x, mix, p["ln_xa_g"].reshape(1, D), p["xa_w_q"].astype(BF16), p["xa_w_o"].astype(BF16), mem_k, mem_v)
    return x2


def _ffn_final(x, p, final_g):
    B, S, D = x.shape
    tm = min(FFN_ROWS, S)
    F2 = p["ffn_w_up"].shape[1]
    F = F2 // 2
    assert F % FFN_COLS == 0 and S % tm == 0
    cw = jnp.concatenate([p["ffn_conv_w"].astype(F32), p["ffn_conv_b"].astype(F32)[None, :]], axis=0)
    return pl.pallas_call(
        _ffn_kernel,
        grid=(B, S // tm),
        in_specs=[pl.BlockSpec((None, tm, D), lambda b, i: (b, i, 0)),
                  _const_spec((1, D)), _const_spec((1, D)), _const_spec((D, F2)),
                  _const_spec((4, F2)), _const_spec((F, D))],
        out_specs=pl.BlockSpec((None, tm, D), lambda b, i: (b, i, 0)),
        out_shape=jax.ShapeDtypeStruct((B, S, D), F32),
        scratch_shapes=[pltpu.VMEM((8, F2), F32), pltpu.VMEM((tm, F), BF16)],
        compiler_params=pltpu.CompilerParams(dimension_semantics=("arbitrary", "arbitrary"),
                                             vmem_limit_bytes=VMEM_LIMIT["conv_ffn"]),
        name="conv_ffn",
    )(x, p["ln_ffn_g"].reshape(1, D), final_g.reshape(1, D), p["ffn_w_up"].astype(BF16), cw,
      p["ffn_w_down"].astype(BF16))


def kernel(x, mem, mem_norm_g, ln_mix_g, w_in, w_out, s5_lam_re, s5_lam_im, s5_log_dt, s5_b_re, s5_b_im, s5_c_re, s5_c_im, s5_d, s5_w_glu, s5_b_glu, lru_conv_w, lru_conv_b, lru_w_a, lru_b_a, lru_w_x, lru_b_x, lru_lam, ln_xa_g, xa_w_q, xa_w_k, xa_w_v, xa_w_o, ln_ffn_g, ffn_w_up, ffn_conv_w, ffn_conv_b, ffn_w_down, final_norm_g):
    params = dict(ln_mix_g=ln_mix_g, w_in=w_in, w_out=w_out, s5_lam_re=s5_lam_re, s5_lam_im=s5_lam_im,
                  s5_log_dt=s5_log_dt, s5_b_re=s5_b_re, s5_b_im=s5_b_im, s5_c_re=s5_c_re, s5_c_im=s5_c_im,
                  s5_d=s5_d, s5_w_glu=s5_w_glu, s5_b_glu=s5_b_glu, lru_conv_w=lru_conv_w,
                  lru_conv_b=lru_conv_b, lru_w_a=lru_w_a, lru_b_a=lru_b_a, lru_w_x=lru_w_x, lru_b_x=lru_b_x,
                  lru_lam=lru_lam, ln_xa_g=ln_xa_g, xa_w_q=xa_w_q, xa_w_k=xa_w_k, xa_w_v=xa_w_v,
                  xa_w_o=xa_w_o, ln_ffn_g=ln_ffn_g, ffn_w_up=ffn_w_up, ffn_conv_w=ffn_conv_w,
                  ffn_conv_b=ffn_conv_b, ffn_w_down=ffn_w_down)
    depth = ln_mix_g.shape[0]
    assert depth == 1, "the ConvFFN kernel fuses the final norm, which needs a single layer"
    B, M, D = mem.shape
    p = {k: v[0] for k, v in params.items()}
    rows = B * M
    tr = min(MEM_ROWS, rows)
    assert rows % tr == 0
    mem_k, mem_v = pl.pallas_call(
        _mem_kv_kernel,
        grid=(rows // tr,),
        in_specs=[pl.BlockSpec((tr, D), lambda i: (i, 0)),
                  _const_spec((1, D)), _const_spec((D, D)), _const_spec((D, D))],
        out_specs=[pl.BlockSpec((tr, D), lambda i: (i, 0)), pl.BlockSpec((tr, D), lambda i: (i, 0))],
        out_shape=[jax.ShapeDtypeStruct((rows, D), BF16), jax.ShapeDtypeStruct((rows, D), BF16)],
        compiler_params=pltpu.CompilerParams(dimension_semantics=("arbitrary",),
                                             vmem_limit_bytes=VMEM_LIMIT["mem_kv"]),
        name="mem_kv",
    )(mem.reshape(rows, D), mem_norm_g.reshape(1, D), p["xa_w_k"].astype(BF16), p["xa_w_v"].astype(BF16))
    x2 = _layer(x, mem_k.reshape(B, M, D), mem_v.reshape(B, M, D), p)
    return _ffn_final(x2, p, final_norm_g)
```
